```python
import math
import jax, jax.numpy as jnp
from jax import lax
import numpy as np

D_MODEL = 1024
BATCH = 16
SEQ = 2048
DEPTH = 1

GDN_HEADS = 8
GDN_DK = 128
GDN_DV = 128
GDN_CONV = 4
GDN_CHUNK = 64
DIFF_HEADS = 8
DIFF_D = 64
DIFF_DV = 2 * DIFF_D
Q_BLOCK = 128
ROPE_THETA = 500000.0
ROPE_DIM = DIFF_D // 4
D_FF = 2816
FFN_CONV = 3
EPS = 1e-6

GDN_QK = GDN_HEADS * GDN_DK
GDN_V = GDN_HEADS * GDN_DV
DIFF_QK = DIFF_HEADS * 2 * DIFF_D
DIFF_V = DIFF_HEADS * DIFF_DV
SPLIT_SIZES = (GDN_QK, GDN_QK, GDN_V, GDN_V, GDN_HEADS, GDN_HEADS,
               DIFF_QK, DIFF_QK, DIFF_V, 2 * D_MODEL)
D_IN = GDN_QK * 2 + GDN_V * 2 + GDN_HEADS * 2 + DIFF_QK * 2 + DIFF_V + 2 * D_MODEL

kernel_name = "hybrid_gdn_diffattn_convglu"


def rmsnorm(x, w):
    xf = x.astype(jnp.float32)
    y = xf * lax.rsqrt(jnp.mean(xf * xf, axis=-1, keepdims=True) + EPS) * w.astype(jnp.float32)
    return y.astype(x.dtype)


def l2norm(x):
    xf = x.astype(jnp.float32)
    return xf * lax.rsqrt(jnp.sum(xf * xf, axis=-1, keepdims=True) + EPS)


def split_cols(u, sizes):
    offsets = [int(o) for o in np.cumsum(np.array(sizes))[:-1]]
    return jnp.split(u, offsets, axis=-1)


def causal_dwconv(x, w):
    K, C = w.shape
    return lax.conv_general_dilated(
        x, w[:, None, :].astype(x.dtype), window_strides=(1,), padding=[(K - 1, 0)],
        dimension_numbers=("NWC", "WIO", "NWC"), feature_group_count=C)


def partial_rope(x, cos, sin):
    half = ROPE_DIM // 2
    xf = x.astype(jnp.float32)
    x1, x2, rest = xf[..., :half], xf[..., half:ROPE_DIM], xf[..., ROPE_DIM:]
    c = cos[None, :, None, None, :]
    s = sin[None, :, None, None, :]
    return jnp.concatenate([x1 * c - x2 * s, x2 * c + x1 * s, rest], axis=-1).astype(x.dtype)


def gated_delta_rule(q, k, v, g, beta):
    B, T, H, dk = q.shape
    dv = v.shape[-1]
    C = GDN_CHUNK
    N = T // C
    f32 = jnp.float32
    q = q.astype(f32) * (dk ** -0.5)
    k, v, g, beta = k.astype(f32), v.astype(f32), g.astype(f32), beta.astype(f32)

    def chunk(a):
        a = a.reshape((B, N, C, H) + a.shape[3:])
        return jnp.moveaxis(a, 3, 1)

    qc, kc, vc = chunk(q), chunk(k), chunk(v)
    gc = jnp.cumsum(chunk(g), axis=-1)
    bc = chunk(beta)[..., None]
    kb, vb = kc * bc, vc * bc

    tri = jnp.tril(jnp.ones((C, C), dtype=bool))
    strict = jnp.tril(jnp.ones((C, C), dtype=bool), -1)
    diff = gc[..., :, None] - gc[..., None, :]
    decay = jnp.exp(jnp.where(tri, diff, -jnp.inf))

    L = jnp.where(strict, jnp.einsum("bhncd,bhnsd->bhncs", kb, kc) * decay, 0.0)
    A = L + jnp.eye(C, dtype=f32)
    u = lax.linalg.triangular_solve(A, vb, left_side=True, lower=True, unit_diagonal=True)
    w = lax.linalg.triangular_solve(A, kb * jnp.exp(gc)[..., None], left_side=True,
                                    lower=True, unit_diagonal=True)
    intra = jnp.einsum("bhncd,bhnsd->bhncs", qc, kc) * decay
    g_last = gc[..., -1]
    q_dec = qc * jnp.exp(gc)[..., None]
    k_dec = kc * jnp.exp(g_last[..., None] - gc)[..., None]

    xs = tuple(jnp.moveaxis(a, 2, 0) for a in (q_dec, k_dec, u, w, intra, jnp.exp(g_last)))

    def step(S, inp):
        qe, kd, u_i, w_i, a_i, dl = inp
        v_new = u_i - jnp.einsum("bhcd,bhde->bhce", w_i, S)
        o = jnp.einsum("bhcd,bhde->bhce", qe, S) + jnp.einsum("bhcs,bhse->bhce", a_i, v_new)
        S = S * dl[..., None, None] + jnp.einsum("bhcd,bhce->bhde", kd, v_new)
        return S, o

    S0 = jnp.zeros((B, H, dk, dv), f32)
    _, o = lax.scan(step, S0, xs)
    return jnp.transpose(o, (1, 0, 3, 2, 4)).reshape(B, T, H, dv)


def gdn_branch(q, k, v, z, b, a, conv_w, A_log, dt_bias, norm_w):
    B, T, _ = q.shape
    qkv = jax.nn.silu(causal_dwconv(jnp.concatenate([q, k, v], axis=-1), conv_w))
    q, k, v = split_cols(qkv, (GDN_QK, GDN_QK, GDN_V))
    q = l2norm(q.reshape(B, T, GDN_HEADS, GDN_DK))
    k = l2norm(k.reshape(B, T, GDN_HEADS, GDN_DK))
    v = v.reshape(B, T, GDN_HEADS, GDN_DV)
    beta = jax.nn.sigmoid(b.astype(jnp.float32))
    g = -jnp.exp(A_log.astype(jnp.float32)) * jax.nn.softplus(
        a.astype(jnp.float32) + dt_bias.astype(jnp.float32))
    o = gated_delta_rule(q, k, v, g, beta)
    o = rmsnorm(o, norm_w) * jax.nn.silu(z.reshape(B, T, GDN_HEADS, GDN_DV).astype(jnp.float32))
    return o.reshape(B, T, GDN_V).astype(z.dtype)


def diff_attention_core(q, k, v, lam):
    B, T, H, _, d = q.shape
    nb = T // Q_BLOCK
    scale = d ** -0.5
    qb = jnp.moveaxis(q.reshape(B, nb, Q_BLOCK, H, 2, d), 1, 0)
    kpos = jnp.arange(T)

    def block(args):
        qi, i = args
        s = jnp.einsum("bqhmd,bkhmd->bhmqk", qi, k,
                       preferred_element_type=jnp.float32) * scale
        qpos = i * Q_BLOCK + jnp.arange(Q_BLOCK)
        mask = kpos[None, :] <= qpos[:, None]
        p = jax.nn.softmax(jnp.where(mask, s, -jnp.inf), axis=-1)
        o = jnp.einsum("bhmqk,bkhe->bqhme", p, v.astype(jnp.float32))
        return o[..., 0, :] - lam * o[..., 1, :]

    o = lax.map(block, (qb, jnp.arange(nb)))
    return jnp.moveaxis(o, 0, 1).reshape(B, T, H, v.shape[-1])


def diff_branch(q, k, v, q_norm_w, k_norm_w, lq1, lk1, lq2, lk2, subln_w, lambda_init):
    B, T, _ = q.shape
    pos = jnp.arange(T, dtype=jnp.float32)
    inv_freq = ROPE_THETA ** (-jnp.arange(0, ROPE_DIM, 2, dtype=jnp.float32) / ROPE_DIM)
    ang = pos[:, None] * inv_freq[None, :]
    cos, sin = jnp.cos(ang), jnp.sin(ang)
    q = partial_rope(rmsnorm(q.reshape(B, T, DIFF_HEADS, 2, DIFF_D), q_norm_w), cos, sin)
    k = partial_rope(rmsnorm(k.reshape(B, T, DIFF_HEADS, 2, DIFF_D), k_norm_w), cos, sin)
    v = v.reshape(B, T, DIFF_HEADS, DIFF_DV)
    f32 = jnp.float32
    lam = (jnp.exp(jnp.sum(lq1.astype(f32) * lk1.astype(f32)))
           - jnp.exp(jnp.sum(lq2.astype(f32) * lk2.astype(f32))) + lambda_init)
    o = diff_attention_core(q, k, v, lam)
    o = rmsnorm(o, subln_w) * (1.0 - lambda_init)
    return o.reshape(B, T, DIFF_V).astype(q.dtype)


def conv_glu_ffn(h, w_up, conv_w, conv_b, w_down):
    u = h @ w_up
    u = causal_dwconv(u, conv_w) + conv_b
    gate, val = jnp.split(u, 2, axis=-1)
    return (jax.nn.silu(gate) * val) @ w_down


def setup_inputs(seed: int = 0) -> dict:
    key = jax.random.key(seed)
    ks = jax.random.split(key, 24)
    f32 = jnp.float32
    L = DEPTH

    def nrm(k, shape, fan_in):
        return jax.random.normal(k, shape, f32) * (fan_in ** -0.5)

    def gain(k, shape):
        return 1.0 + 0.02 * jax.random.normal(k, shape, f32)

    dt = jnp.exp(jax.random.uniform(ks[6], (L, GDN_HEADS), f32, math.log(1e-3), math.log(1e-1)))
    return {
        "x": jax.random.normal(ks[0], (BATCH, SEQ, D_MODEL), f32),
        "norm1_w": gain(ks[1], (L, D_MODEL)),
        "w_in": nrm(ks[2], (L, D_MODEL, D_IN), D_MODEL),
        "b_gate": 0.02 * jax.random.normal(ks[3], (L, 2 * D_MODEL), f32),
        "gdn_conv_w": nrm(ks[4], (L, GDN_CONV, GDN_QK * 2 + GDN_V), GDN_CONV),
        "gdn_A_log": jnp.log(jax.random.uniform(ks[5], (L, GDN_HEADS), f32, 1.0, 16.0)),
        "gdn_dt_bias": dt + jnp.log(-jnp.expm1(-dt)),
        "gdn_norm_w": gain(ks[7], (L, GDN_DV)),
        "diff_q_norm_w": gain(ks[8], (L, DIFF_D)),
        "diff_k_norm_w": gain(ks[9], (L, DIFF_D)),
        "lambda_q1": 0.1 * jax.random.normal(ks[10], (L, DIFF_D), f32),
        "lambda_k1": 0.1 * jax.random.normal(ks[11], (L, DIFF_D), f32),
        "lambda_q2": 0.1 * jax.random.normal(ks[12], (L, DIFF_D), f32),
        "lambda_k2": 0.1 * jax.random.normal(ks[13], (L, DIFF_D), f32),
        "diff_subln_w": gain(ks[14], (L, DIFF_DV)),
        "w_gdn_out": nrm(ks[15], (L, GDN_V, D_MODEL), GDN_V),
        "w_diff_out": nrm(ks[16], (L, DIFF_V, D_MODEL), DIFF_V),
        "w_o": nrm(ks[17], (L, D_MODEL, D_MODEL), D_MODEL),
        "norm2_w": gain(ks[18], (L, D_MODEL)),
        "w_up": nrm(ks[19], (L, D_MODEL, 2 * D_FF), D_MODEL),
        "ffn_conv_w": nrm(ks[20], (L, FFN_CONV, 2 * D_FF), FFN_CONV),
        "ffn_conv_b": 0.02 * jax.random.normal(ks[21], (L, 2 * D_FF), f32),
        "w_down": nrm(ks[22], (L, D_FF, D_MODEL), D_FF),
    }


def reference(x, norm1_w, w_in, b_gate, gdn_conv_w, gdn_A_log, gdn_dt_bias, gdn_norm_w,
              diff_q_norm_w, diff_k_norm_w, lambda_q1, lambda_k1, lambda_q2, lambda_k2,
              diff_subln_w, w_gdn_out, w_diff_out, w_o, norm2_w, w_up, ffn_conv_w,
              ffn_conv_b, w_down):
    for layer in range(DEPTH):
        lambda_init = 0.8 - 0.6 * math.exp(-0.3 * layer)
        h = rmsnorm(x, norm1_w[layer])
        u = h @ w_in[layer]
        (g_q, g_k, g_v, g_z, g_b, g_a, d_q, d_k, d_v, gates) = split_cols(u, SPLIT_SIZES)
        y_a = gdn_branch(g_q, g_k, g_v, g_z, g_b, g_a, gdn_conv_w[layer], gdn_A_log[layer],
                         gdn_dt_bias[layer], gdn_norm_w[layer]) @ w_gdn_out[layer]
        y_b = diff_branch(d_q, d_k, d_v, diff_q_norm_w[layer], diff_k_norm_w[layer],
                          lambda_q1[layer], lambda_k1[layer], lambda_q2[layer], lambda_k2[layer],
                          diff_subln_w[layer], lambda_init) @ w_diff_out[layer]
        gate_a, gate_b = jnp.split(jax.nn.sigmoid(gates + b_gate[layer]), 2, axis=-1)
        x = x + (gate_a * y_a + gate_b * y_b) @ w_o[layer]
        h = rmsnorm(x, norm2_w[layer])
        x = x + conv_glu_ffn(h, w_up[layer], ffn_conv_w[layer], ffn_conv_b[layer], w_down[layer])
    return x
```

```python
import functools
import math

import jax
import jax.numpy as jnp
from jax import lax
from jax.experimental import pallas as pl
from jax.experimental.pallas import tpu as pltpu

F32 = jnp.float32
BF16 = jnp.bfloat16

EPS = 1e-6
LANES = 128
GDN_HEADS = 8
GDN_D = 128
GDN_CONV = 4
CHUNK = 64
GROUP = 256
DIFF_HEADS = 8
DIFF_D = 64
ROPE_DIM = DIFF_D // 4
ROPE_THETA = 500000.0
FFN_CONV = 3
VMEM_LIMIT = 56 * 1024 * 1024


def _mm(a, b):
    return jnp.dot(a.astype(BF16), b.astype(BF16), preferred_element_type=F32)


def _mm_nt(a, b):
    return lax.dot_general(a.astype(BF16), b.astype(BF16), (((1,), (1,)), ((), ())),
                           preferred_element_type=F32)


def _mm_tn(a, b):
    return lax.dot_general(a.astype(BF16), b.astype(BF16), (((0,), (0,)), ((), ())),
                           preferred_element_type=F32)


def _sigmoid(x):
    return 1.0 / (1.0 + jnp.exp(-x))


def _silu(x):
    return x * _sigmoid(x)


def _shift_rows(cur, prev8, s):
    rc = pltpu.roll(cur, s, axis=0)
    rp = pltpu.roll(prev8, s, axis=0)
    row = lax.broadcasted_iota(jnp.int32, prev8.shape, 0)
    first = jnp.where(row < s, rp, rc[0:8])
    return jnp.concatenate([first, rc[8:]], axis=0)


def _inproj_body(x_ref, n1_ref, w_ref, wba_ref, gp_ref, u_ref, bgc_ref, gt_ref, h_scr):
    j = pl.program_id(1)

    @pl.when(j == 0)
    def _():
        x = x_ref[...]
        ms = jnp.mean(x * x, axis=-1, keepdims=True)
        hb = (x * lax.rsqrt(ms + EPS) * n1_ref[...]).astype(BF16)
        h_scr[...] = hb
        ba = jnp.dot(hb, wba_ref[...], preferred_element_type=F32)
        tm = ba.shape[0]
        lane = lax.broadcasted_iota(jnp.int32, ba.shape, 1)
        row = lax.broadcasted_iota(jnp.int32, ba.shape, 0)
        beta = _sigmoid(ba)
        zz = ba + gp_ref[1:2, :]
        softplus = jnp.maximum(zz, 0.0) + jnp.log1p(jnp.exp(-jnp.abs(zz)))
        g = -jnp.exp(gp_ref[0:1, :]) * softplus
        rin = row & (CHUNK - 1)
        gc = g
        s = 1
        while s < CHUNK:
            gc = gc + jnp.where(rin >= s, pltpu.roll(gc, s, axis=0), 0.0)
            s *= 2
        glast = jnp.broadcast_to(
            gc.reshape(tm // CHUNK, CHUNK, LANES)[:, CHUNK - 1:CHUNK, :],
            (tm // CHUNK, CHUNK, LANES)).reshape(tm, LANES)
        glast = pltpu.roll(glast, GDN_HEADS, axis=1)
        out = jnp.where(lane < GDN_HEADS, beta,
                        jnp.where(lane < 2 * GDN_HEADS, gc,
                                  jnp.where(lane < 3 * GDN_HEADS, glast, 0.0)))
        bgc_ref[...] = out
        gt_ref[0] = out.T[GDN_HEADS:2 * GDN_HEADS, :]

    u_ref[...] = jnp.dot(h_scr[...], w_ref[...], preferred_element_type=F32).astype(BF16)


def _inproj(x2d, n1, w, wba, gp, seq, tm, tn):
    m, d = x2d.shape
    n = w.shape[1]
    tpb = seq // tm
    return pl.pallas_call(
        _inproj_body,
        grid=(m // tm, n // tn),
        in_specs=[
            pl.BlockSpec((tm, d), lambda i, j: (i, 0)),
            pl.BlockSpec((1, d), lambda i, j: (0, 0)),
            pl.BlockSpec((d, tn), lambda i, j: (0, j)),
            pl.BlockSpec((d, LANES), lambda i, j: (0, 0)),
            pl.BlockSpec((2, LANES), lambda i, j: (0, 0)),
        ],
        out_specs=[
            pl.BlockSpec((tm, tn), lambda i, j: (i, j)),
            pl.BlockSpec((tm, LANES), lambda i, j: (i, 0)),
            pl.BlockSpec((1, GDN_HEADS, tm), lambda i, j: (i // tpb, 0, i % tpb)),
        ],
        out_shape=[
            jax.ShapeDtypeStruct((m, n), BF16),
            jax.ShapeDtypeStruct((m, LANES), F32),
            jax.ShapeDtypeStruct((m // seq, GDN_HEADS, seq), F32),
        ],
        scratch_shapes=[pltpu.VMEM((tm, d), BF16)],
        compiler_params=pltpu.CompilerParams(
            dimension_semantics=("arbitrary", "arbitrary"), vmem_limit_bytes=VMEM_LIMIT),
        name="inproj",
    )(x2d, n1, w, wba, gp)


def _gdn_body(q_ref, k_ref, v_ref, z_ref, cwq_ref, cwk_ref, cwv_ref, bgc_ref, gt_ref, nw_ref,
              o_ref, m_scr, b_scr, qe_scr, o0_scr, dl_scr, *, hb, seq):
    hbase = pl.program_id(1) * hb
    ngroups = seq // GROUP
    nchunks = seq // CHUNK
    cpg = GROUP // CHUNK

    ri = lax.broadcasted_iota(jnp.int32, (GROUP, GROUP), 0)
    ci = lax.broadcasted_iota(jnp.int32, (GROUP, GROUP), 1)
    tri = ci <= ri
    strict = ci < ri
    same8 = (ri >> 3) == (ci >> 3)
    same16 = (ri >> 4) == (ci >> 4)
    same32 = (ri >> 5) == (ci >> 5)
    same64 = (ri >> 6) == (ci >> 6)
    eye = (ri == ci).astype(F32)
    lane = lax.broadcasted_iota(jnp.int32, (GROUP, LANES), 1)

    def conv_silu(ref, cw_ref, r0, g, lanes):
        cur = ref[0, pl.ds(r0, GROUP), lanes].astype(F32)
        pstart = pl.multiple_of(jnp.maximum(r0 - 16, 0), 16)
        prev = ref[0, pl.ds(pstart, 16), lanes].astype(F32)[8:16]
        prev = jnp.where(g > 0, prev, 0.0)
        w = cw_ref[:, lanes]
        acc = cur * w[GDN_CONV - 1:GDN_CONV]
        for s in range(1, GDN_CONV):
            acc = acc + _shift_rows(cur, prev, s) * w[GDN_CONV - 1 - s:GDN_CONV - s]
        return _silu(acc)

    def l2n(x):
        return x * lax.rsqrt(jnp.sum(x * x, axis=-1, keepdims=True) + EPS)

    def group_step(g, carry):
        r0 = pl.multiple_of(g * GROUP, GROUP)
        blk = bgc_ref[0, pl.ds(r0, GROUP), :]
        for hh in range(hb):
            lanes = slice(hh * LANES, (hh + 1) * LANES)
            head = hbase + hh
            qn = l2n(conv_silu(q_ref, cwq_ref, r0, g, lanes)) * (GDN_D ** -0.5)
            kn = l2n(conv_silu(k_ref, cwk_ref, r0, g, lanes))
            vn = conv_silu(v_ref, cwv_ref, r0, g, lanes)

            def col(off):
                return jnp.sum(jnp.where(lane == head + off, blk, 0.0), axis=-1, keepdims=True)

            beta = col(0)
            gc = col(GDN_HEADS)
            glast = col(2 * GDN_HEADS)
            gc_row = gt_ref[0, pl.ds(head, 1), pl.ds(r0, GROUP)]
            eg = jnp.exp(gc)
            kb = kn * beta
            vb = vn * beta
            kbg = kb * eg
            q_dec = qn * eg
            k_dec = kn * jnp.exp(glast - gc)

            kn_b = kn.astype(BF16)
            kk = _mm_nt(kb, kn_b)
            qk = _mm_nt(qn, kn_b)
            decay = jnp.exp(jnp.where(same64 & tri, gc - gc_row, -1e30))
            lmat = jnp.where(strict, kk * decay, 0.0)
            intra = qk * decay

            d8 = jnp.where(same8, lmat, 0.0)
            d8_2 = _mm(d8, d8)
            d8_4 = _mm(d8_2, d8_2)
            x = eye - d8
            x = x + _mm(x, d8_2)
            x = x + _mm(x, d8_4)
            for inner, outer in ((same8, same16), (same16, same32), (same32, same64)):
                off = jnp.where(outer & jnp.logical_not(inner), lmat, 0.0)
                x = x - _mm(x, _mm(off, x))

            uw = _mm(x, jnp.concatenate([vb, kbg], axis=1))
            iu = _mm(intra, uw)
            rows = pl.ds(r0, GROUP)
            o0_scr[hh, rows, :] = iu[:, :GDN_D]
            qe_scr[hh, rows, :] = (q_dec - iu[:, GDN_D:]).astype(BF16)
            uw_b = uw.astype(BF16)
            for c in range(cpg):
                cs = slice(c * CHUNK, (c + 1) * CHUNK)
                bm = _mm_tn(k_dec[cs], uw_b[cs])
                n = g * cpg + c
                b_scr[hh, n] = bm[:, :GDN_D]
                m_scr[hh, n] = bm[:, GDN_D:].astype(BF16)
                dl_scr[hh, n] = jnp.broadcast_to(
                    jnp.exp(glast[c * CHUNK:c * CHUNK + 8]), (8, LANES))
        return carry

    lax.fori_loop(0, ngroups, group_step, 0)

    nw = nw_ref[...]

    def chunk_step(n, states):
        rows = pl.ds(pl.multiple_of(n * CHUNK, CHUNK), CHUNK)
        new_states = []
        for hh in range(hb):
            lanes = slice(hh * LANES, (hh + 1) * LANES)
            s = states[hh]
            sb = s.astype(BF16)
            o = jnp.dot(qe_scr[hh, rows, :], sb, preferred_element_type=F32) + o0_scr[hh, rows, :]
            ms = jnp.mean(o * o, axis=-1, keepdims=True)
            zz = z_ref[0, rows, lanes].astype(F32)
            o_ref[0, rows, lanes] = (o * lax.rsqrt(ms + EPS) * nw * _silu(zz)).astype(BF16)
            s = (s * dl_scr[hh, n][0:1, :]
                 - jnp.dot(m_scr[hh, n], sb, preferred_element_type=F32) + b_scr[hh, n])
            new_states.append(s)
        return tuple(new_states)

    init = tuple(jnp.zeros((GDN_D, GDN_D), F32) for _ in range(hb))
    lax.fori_loop(0, nchunks, chunk_step, init)


def _gdn(u3, conv_w, bgc3, gt, norm_w, hb):
    b, seq, _ = u3.shape
    w = hb * LANES
    nblk = GDN_HEADS // hb
    nchunks = seq // CHUNK
    body = functools.partial(_gdn_body, hb=hb, seq=seq)

    def tok(col0):
        return pl.BlockSpec((1, seq, w), lambda i, j: (i, 0, col0 + j))

    def cw(col0):
        return pl.BlockSpec((GDN_CONV, w), lambda i, j: (0, col0 + j))

    return pl.pallas_call(
        body,
        grid=(b, nblk),
        in_specs=[
            tok(0), tok(nblk), tok(2 * nblk), tok(3 * nblk),
            cw(0), cw(nblk), cw(2 * nblk),
            pl.BlockSpec((1, seq, LANES), lambda i, j: (i, 0, 0)),
            pl.BlockSpec((1, GDN_HEADS, seq), lambda i, j: (i, 0, 0)),
            pl.BlockSpec((1, GDN_D), lambda i, j: (0, 0)),
        ],
        out_specs=pl.BlockSpec((1, seq, w), lambda i, j: (i, 0, j)),
        out_shape=jax.ShapeDtypeStruct((b, seq, GDN_HEADS * GDN_D), BF16),
        scratch_shapes=[
            pltpu.VMEM((hb, nchunks, GDN_D, GDN_D), BF16),
            pltpu.VMEM((hb, nchunks, GDN_D, GDN_D), F32),
            pltpu.VMEM((hb, seq, GDN_D), BF16),
            pltpu.VMEM((hb, seq, GDN_D), F32),
            pltpu.VMEM((hb, nchunks, 8, LANES), F32),
        ],
        compiler_params=pltpu.CompilerParams(
            dimension_semantics=("arbitrary", "arbitrary"), vmem_limit_bytes=VMEM_LIMIT),
        name="gdn",
    )(u3, u3, u3, u3, conv_w, conv_w, conv_w, bgc3, gt, norm_w)


def _attn_body(q_ref, k_ref, v_ref, cq_ref, saq_ref, sbq_ref, ck_ref, sak_ref, sbk_ref,
               qnw_ref, knw_ref, lam_ref, sw_ref, o_ref, k_scr, *, seq, tq, lambda_init):
    qi = pl.program_id(2)
    lane_t = lax.broadcasted_iota(jnp.int32, (tq, LANES), 1)
    low = lane_t < DIFF_D

    def norm_rope(x, w, c, sa, sb):
        x2 = x * x
        lo = jnp.sum(jnp.where(low, x2, 0.0), axis=-1, keepdims=True)
        hi = jnp.sum(jnp.where(low, 0.0, x2), axis=-1, keepdims=True)
        ms = jnp.where(low, lo, hi) * (1.0 / DIFF_D)
        y = x * lax.rsqrt(ms + EPS) * w
        half = ROPE_DIM // 2
        return y * c + pltpu.roll(y, LANES - half, axis=1) * sa + pltpu.roll(y, half, axis=1) * sb

    @pl.when(qi == 0)
    def _():
        for t in range(seq // tq):
            rows = slice(t * tq, (t + 1) * tq)
            kx = k_ref[0, rows, :].astype(F32)
            k_scr[rows, :] = norm_rope(kx, knw_ref[...], ck_ref[rows, :], sak_ref[rows, :],
                                       sbk_ref[rows, :]).astype(BF16)

    qx = norm_rope(q_ref[0].astype(F32), qnw_ref[...], cq_ref[...], saq_ref[...], sbq_ref[...])
    qx = qx * (DIFF_D ** -0.5)
    q1 = jnp.where(low, qx, 0.0).astype(BF16)
    q2 = jnp.where(low, 0.0, qx).astype(BF16)

    def tile(j, carry, masked):
        rows = pl.ds(pl.multiple_of(j * tq, tq), tq)
        kt = k_scr[rows, :]
        vt = v_ref[0, rows, :]
        out = []
        for qm, (m, l, acc) in zip((q1, q2), carry):
            s = lax.dot_general(qm, kt, (((1,), (1,)), ((), ())), preferred_element_type=F32)
            if masked:
                r = lax.broadcasted_iota(jnp.int32, s.shape, 0)
                c = lax.broadcasted_iota(jnp.int32, s.shape, 1)
                s = jnp.where(c <= r, s, -1e30)
            m_new = jnp.maximum(m, jnp.max(s, axis=-1, keepdims=True))
            p = jnp.exp(s - m_new)
            alpha = jnp.exp(m - m_new)
            l = alpha * l + jnp.sum(p, axis=-1, keepdims=True)
            acc = alpha * acc + jnp.dot(p.astype(BF16), vt, preferred_element_type=F32)
            out.append((m_new, l, acc))
        return tuple(out)

    def fresh():
        return (jnp.full((tq, 1), -1e30, F32), jnp.zeros((tq, 1), F32), jnp.zeros((tq, LANES), F32))

    carry = lax.fori_loop(0, qi, lambda j, c: tile(j, c, False), (fresh(), fresh()))
    (_, l1, a1), (_, l2, a2) = tile(qi, carry, True)

    lam = (jnp.exp(jnp.sum(lam_ref[0:1, :] * lam_ref[1:2, :], axis=-1, keepdims=True))
           - jnp.exp(jnp.sum(lam_ref[2:3, :] * lam_ref[3:4, :], axis=-1, keepdims=True))
           + lambda_init)
    o = a1 / l1 - lam * (a2 / l2)
    ms = jnp.mean(o * o, axis=-1, keepdims=True)
    o_ref[0] = (o * lax.rsqrt(ms + EPS) * sw_ref[...] * (1.0 - lambda_init)).astype(BF16)


def _attn(u3, rope_tabs, qnw, knw, lam4, subln_w, lambda_init, tq):
    b, seq, _ = u3.shape
    c_t, sa_t, sb_t = rope_tabs
    body = functools.partial(_attn_body, seq=seq, tq=tq, lambda_init=lambda_init)
    qcol, kcol, vcol = 4 * GDN_HEADS, 4 * GDN_HEADS + DIFF_HEADS, 4 * GDN_HEADS + 2 * DIFF_HEADS
    tabq = pl.BlockSpec((tq, LANES), lambda i, h, t: (t, 0))
    tabk = pl.BlockSpec((seq, LANES), lambda i, h, t: (0, 0))
    vec = pl.BlockSpec((1, LANES), lambda i, h, t: (0, 0))
    return pl.pallas_call(
        body,
        grid=(b, DIFF_HEADS, seq // tq),
        in_specs=[
            pl.BlockSpec((1, tq, LANES), lambda i, h, t: (i, t, qcol + h)),
            pl.BlockSpec((1, seq, LANES), lambda i, h, t: (i, 0, kcol + h)),
            pl.BlockSpec((1, seq, LANES), lambda i, h, t: (i, 0, vcol + h)),
            tabq, tabq, tabq, tabk, tabk, tabk,
            vec, vec,
            pl.BlockSpec((4, LANES), lambda i, h, t: (0, 0)),
            vec,
        ],
        out_specs=pl.BlockSpec((1, tq, LANES), lambda i, h, t: (i, t, h)),
        out_shape=jax.ShapeDtypeStruct((b, seq, DIFF_HEADS * 2 * DIFF_D), BF16),
        scratch_shapes=[pltpu.VMEM((seq, LANES), BF16)],
        compiler_params=pltpu.CompilerParams(
            dimension_semantics=("arbitrary", "arbitrary", "arbitrary"),
            vmem_limit_bytes=VMEM_LIMIT),
        name="attn",
    )(u3, u3, u3, c_t, sa_t, sb_t, c_t, sa_t, sb_t, qnw, knw, lam4, subln_w)


def _mixer_body(x_ref, og_ref, od_ref, ga_ref, gb_ref, bg_ref, wg_ref, wd_ref, wo_ref, o_ref):
    d = x_ref.shape[1]
    ya = jnp.dot(og_ref[...], wg_ref[...], preferred_element_type=F32)
    yb = jnp.dot(od_ref[...], wd_ref[...], preferred_element_type=F32)
    ga = _sigmoid(ga_ref[...].astype(F32) + bg_ref[:, :d])
    gb = _sigmoid(gb_ref[...].astype(F32) + bg_ref[:, d:])
    mix = (ga * ya + gb * yb).astype(BF16)
    o_ref[...] = x_ref[...] + jnp.dot(mix, wo_ref[...], preferred_element_type=F32)


def _mixer(x2d, og, od, u, b_gate, wg, wd, wo, gate_col, tm):
    m, d = x2d.shape
    row = lambda i: (i, 0)
    full = lambda i: (0, 0)
    return pl.pallas_call(
        _mixer_body,
        grid=(m // tm,),
        in_specs=[
            pl.BlockSpec((tm, d), row),
            pl.BlockSpec((tm, d), row),
            pl.BlockSpec((tm, d), row),
            pl.BlockSpec((tm, d), lambda i: (i, gate_col)),
            pl.BlockSpec((tm, d), lambda i: (i, gate_col + 1)),
            pl.BlockSpec((1, 2 * d), full),
            pl.BlockSpec((d, d), full),
            pl.BlockSpec((d, d), full),
            pl.BlockSpec((d, d), full),
        ],
        out_specs=pl.BlockSpec((tm, d), row),
        out_shape=jax.ShapeDtypeStruct((m, d), F32),
        compiler_params=pltpu.CompilerParams(
            dimension_semantics=("arbitrary",), vmem_limit_bytes=VMEM_LIMIT),
        name="mixer",
    )(x2d, og, od, u, u, b_gate, wg, wd, wo)


def _ffn_body(x_ref, n2_ref, wup_ref, cw_ref, cb_ref, wdn_ref, o_ref, carry_scr, *, tpb, tf):
    i = pl.program_id(0)
    dff = wdn_ref.shape[0]
    x = x_ref[...]
    ms = jnp.mean(x * x, axis=-1, keepdims=True)
    hb = (x * lax.rsqrt(ms + EPS) * n2_ref[...]).astype(BF16)
    tm = x.shape[0]
    first = (i % tpb) == 0

    def conv(u, cols):
        prev = jnp.where(first, 0.0, carry_scr[:, cols])
        w = cw_ref[:, cols]
        acc = u * w[FFN_CONV - 1:FFN_CONV] + cb_ref[:, cols]
        for s in range(1, FFN_CONV):
            acc = acc + _shift_rows(u, prev, s) * w[FFN_CONV - 1 - s:FFN_CONV - s]
        carry_scr[:, cols] = u[tm - 8:tm]
        return acc

    acc = x
    for f in range(dff // tf):
        gcols = slice(f * tf, (f + 1) * tf)
        vcols = slice(dff + f * tf, dff + (f + 1) * tf)
        ug = jnp.dot(hb, wup_ref[:, gcols], preferred_element_type=F32)
        uv = jnp.dot(hb, wup_ref[:, vcols], preferred_element_type=F32)
        act = (_silu(conv(ug, gcols)) * conv(uv, vcols)).astype(BF16)
        acc = acc + jnp.dot(act, wdn_ref[gcols, :], preferred_element_type=F32)
    o_ref[...] = acc


def _ffn(x2d, n2, wup, cw, cb, wdn, seq, tm, tf):
    m, d = x2d.shape
    dff = wdn.shape[0]
    full = lambda i: (0, 0)
    body = functools.partial(_ffn_body, tpb=seq // tm, tf=tf)
    return pl.pallas_call(
        body,
        grid=(m // tm,),
        in_specs=[
            pl.BlockSpec((tm, d), lambda i: (i, 0)),
            pl.BlockSpec((1, d), full),
            pl.BlockSpec((d, 2 * dff), full, pipeline_mode=pl.Buffered(1)),
            pl.BlockSpec((FFN_CONV, 2 * dff), full),
            pl.BlockSpec((1, 2 * dff), full),
            pl.BlockSpec((dff, d), full, pipeline_mode=pl.Buffered(1)),
        ],
        out_specs=pl.BlockSpec((tm, d), lambda i: (i, 0)),
        out_shape=jax.ShapeDtypeStruct((m, d), F32),
        scratch_shapes=[pltpu.VMEM((8, 2 * dff), F32)],
        compiler_params=pltpu.CompilerParams(
            dimension_semantics=("arbitrary",), vmem_limit_bytes=VMEM_LIMIT),
        name="ffn",
    )(x2d, n2, wup, cw, cb, wdn)


def _rope_tables(seq):
    half = ROPE_DIM // 2
    pos = jnp.arange(seq, dtype=F32)
    inv_freq = ROPE_THETA ** (-jnp.arange(0, ROPE_DIM, 2, dtype=F32) / ROPE_DIM)
    ang = pos[:, None] * inv_freq[None, :]
    cos, sin = jnp.cos(ang), jnp.sin(ang)
    ones = jnp.ones((seq, DIFF_D - ROPE_DIM), F32)
    zeros = jnp.zeros((seq, DIFF_D - ROPE_DIM), F32)
    zh = jnp.zeros((seq, half), F32)
    c = jnp.concatenate([cos, cos, ones], axis=-1)
    sa = jnp.concatenate([-sin, zh, zeros], axis=-1)
    sb = jnp.concatenate([zh, sin, zeros], axis=-1)
    return tuple(jnp.concatenate([t, t], axis=-1) for t in (c, sa, sb))


def _pad_lanes(v, offset):
    return jnp.zeros((LANES,), F32).at[offset:offset + v.shape[0]].set(v.astype(F32))


def kernel(x, norm1_w, w_in, b_gate, gdn_conv_w, gdn_A_log, gdn_dt_bias, gdn_norm_w,
           diff_q_norm_w, diff_k_norm_w, lambda_q1, lambda_k1, lambda_q2, lambda_k2,
           diff_subln_w, w_gdn_out, w_diff_out, w_o, norm2_w, w_up, ffn_conv_w,
           ffn_conv_b, w_down):
    b, seq, d = x.shape
    depth = norm1_w.shape[0]
    gqk = GDN_HEADS * GDN_D
    dqk = DIFF_HEADS * 2 * DIFF_D
    o_b = 4 * gqk
    o_dq = o_b + 2 * GDN_HEADS
    o_gate = o_dq + 3 * dqk
    dff = w_down.shape[1]
    rope_tabs = _rope_tables(seq)
    x2d = x.reshape(b * seq, d)
    tm_in = min(1024, seq)
    tq = min(512, seq)
    tm_mix = min(512, seq)
    tm_ffn = min(512, seq)
    tf = dff // 2

    for layer in range(depth):
        lambda_init = 0.8 - 0.6 * math.exp(-0.3 * layer)
        wl = w_in[layer]
        w_main = jnp.concatenate([wl[:, :o_b], wl[:, o_dq:]], axis=1).astype(BF16)
        w_ba = jnp.pad(wl[:, o_b:o_dq], ((0, 0), (0, LANES - 2 * GDN_HEADS))).astype(BF16)
        gp = jnp.stack([_pad_lanes(gdn_A_log[layer], GDN_HEADS),
                        _pad_lanes(gdn_dt_bias[layer], GDN_HEADS)])
        u, bgc, gt = _inproj(x2d, norm1_w[layer][None, :], w_main, w_ba, gp, seq, tm_in, 1024)
        u3 = u.reshape(b, seq, u.shape[1])
        og = _gdn(u3, gdn_conv_w[layer], bgc.reshape(b, seq, LANES), gt,
                  gdn_norm_w[layer][None, :], hb=2)
        qnw = jnp.tile(diff_q_norm_w[layer], 2)[None, :]
        knw = jnp.tile(diff_k_norm_w[layer], 2)[None, :]
        lam4 = jnp.stack([jnp.pad(v[layer], (0, LANES - DIFF_D))
                          for v in (lambda_q1, lambda_k1, lambda_q2, lambda_k2)])
        od = _attn(u3, rope_tabs, qnw, knw, lam4, diff_subln_w[layer][None, :], lambda_init, tq)
        x2d = _mixer(x2d, og.reshape(b * seq, gqk), od.reshape(b * seq, dqk), u,
                     b_gate[layer][None, :], w_gdn_out[layer].astype(BF16),
                     w_diff_out[layer].astype(BF16), w_o[layer].astype(BF16),
                     (4 * gqk + 3 * dqk) // d, tm_mix)
        x2d = _ffn(x2d, norm2_w[layer][None, :], w_up[layer].astype(BF16), ffn_conv_w[layer],
                   ffn_conv_b[layer][None, :], w_down[layer].astype(BF16), seq, tm_ffn, tf)
    return x2d.reshape(b, seq, d)
```

```python
import functools
import math

import jax
import jax.numpy as jnp
from jax import lax
from jax.experimental import pallas as pl
from jax.experimental.pallas import tpu as pltpu

F32 = jnp.float32
BF16 = jnp.bfloat16

EPS = 1e-6
LANES = 128
GDN_HEADS = 8
GDN_D = 128
GDN_CONV = 4
CHUNK = 64
GROUP = 256
DIFF_HEADS = 8
DIFF_D = 64
ROPE_DIM = DIFF_D // 4
ROPE_THETA = 500000.0
FFN_CONV = 3
ONES_ROWS = 16
VMEM_LIMIT = 56 * 1024 * 1024


def _mm(a, b):
    return jnp.dot(a.astype(BF16), b.astype(BF16), preferred_element_type=F32)


def _mm_nt(a, b):
    return lax.dot_general(a.astype(BF16), b.astype(BF16), (((1,), (1,)), ((), ())),
                           preferred_element_type=F32)


def _mm_tn(a, b):
    return lax.dot_general(a.astype(BF16), b.astype(BF16), (((0,), (0,)), ((), ())),
                           preferred_element_type=F32)


def _sigmoid(x):
    return 1.0 / (1.0 + jnp.exp(-x))


def _silu(x):
    return x * _sigmoid(x)


def _shift_rows(cur, prev8, s):
    rc = pltpu.roll(cur, s, axis=0)
    rp = pltpu.roll(prev8, s, axis=0)
    row = lax.broadcasted_iota(jnp.int32, prev8.shape, 0)
    first = jnp.where(row < s, rp, rc[0:8])
    return jnp.concatenate([first, rc[8:]], axis=0)


def _inproj_body(x_ref, n1_ref, w_ref, wba_ref, gp_ref, u_ref, bgc_ref, gt_ref, h_scr):
    j = pl.program_id(1)

    @pl.when(j == 0)
    def _():
        x = x_ref[...]
        ms = jnp.mean(x * x, axis=-1, keepdims=True)
        hb = (x * lax.rsqrt(ms + EPS) * n1_ref[...]).astype(BF16)
        h_scr[...] = hb
        ba = jnp.dot(hb, wba_ref[...], preferred_element_type=F32)
        tm = ba.shape[0]
        lane = lax.broadcasted_iota(jnp.int32, ba.shape, 1)
        row = lax.broadcasted_iota(jnp.int32, ba.shape, 0)
        beta = _sigmoid(ba)
        zz = ba + gp_ref[1:2, :]
        softplus = jnp.maximum(zz, 0.0) + jnp.log1p(jnp.exp(-jnp.abs(zz)))
        g = -jnp.exp(gp_ref[0:1, :]) * softplus
        rin = row & (CHUNK - 1)
        gc = g
        s = 1
        while s < CHUNK:
            gc = gc + jnp.where(rin >= s, pltpu.roll(gc, s, axis=0), 0.0)
            s *= 2
        glast = jnp.broadcast_to(
            gc.reshape(tm // CHUNK, CHUNK, LANES)[:, CHUNK - 1:CHUNK, :],
            (tm // CHUNK, CHUNK, LANES)).reshape(tm, LANES)
        glast = pltpu.roll(glast, GDN_HEADS, axis=1)
        out = jnp.where(lane < GDN_HEADS, beta,
                        jnp.where(lane < 2 * GDN_HEADS, gc,
                                  jnp.where(lane < 3 * GDN_HEADS, glast, 0.0)))
        bgc_ref[...] = out
        gt_ref[0] = out.T[GDN_HEADS:2 * GDN_HEADS, :]

    u_ref[...] = jnp.dot(h_scr[...], w_ref[...], preferred_element_type=F32).astype(BF16)


def _inproj(x2d, n1, w, wba, gp, seq, tm, tn):
    m, d = x2d.shape
    n = w.shape[1]
    tpb = seq // tm
    return pl.pallas_call(
        _inproj_body,
        grid=(m // tm, n // tn),
        in_specs=[
            pl.BlockSpec((tm, d), lambda i, j: (i, 0)),
            pl.BlockSpec((1, d), lambda i, j: (0, 0)),
            pl.BlockSpec((d, tn), lambda i, j: (0, j)),
            pl.BlockSpec((d, LANES), lambda i, j: (0, 0)),
            pl.BlockSpec((2, LANES), lambda i, j: (0, 0)),
        ],
        out_specs=[
            pl.BlockSpec((tm, tn), lambda i, j: (i, j)),
            pl.BlockSpec((tm, LANES), lambda i, j: (i, 0)),
            pl.BlockSpec((1, GDN_HEADS, tm), lambda i, j: (i // tpb, 0, i % tpb)),
        ],
        out_shape=[
            jax.ShapeDtypeStruct((m, n), BF16),
            jax.ShapeDtypeStruct((m, LANES), F32),
            jax.ShapeDtypeStruct((m // seq, GDN_HEADS, seq), F32),
        ],
        scratch_shapes=[pltpu.VMEM((tm, d), BF16)],
        compiler_params=pltpu.CompilerParams(
            dimension_semantics=("arbitrary", "arbitrary"), vmem_limit_bytes=VMEM_LIMIT),
        name="inproj",
    )(x2d, n1, w, wba, gp)


def _gdn_body(q_ref, k_ref, v_ref, z_ref, cwq_ref, cwk_ref, cwv_ref, bgc_ref, gt_ref, nw_ref,
              o_ref, m_scr, b_scr, qe_scr, o0_scr, dl_scr, *, hb, seq):
    hbase = pl.program_id(1) * hb
    ngroups = seq // GROUP
    nchunks = seq // CHUNK
    cpg = GROUP // CHUNK

    ri = lax.broadcasted_iota(jnp.int32, (GROUP, GROUP), 0)
    ci = lax.broadcasted_iota(jnp.int32, (GROUP, GROUP), 1)
    tri = ci <= ri
    strict = ci < ri
    same8 = (ri >> 3) == (ci >> 3)
    same16 = (ri >> 4) == (ci >> 4)
    same32 = (ri >> 5) == (ci >> 5)
    same64 = (ri >> 6) == (ci >> 6)
    eye = (ri == ci).astype(F32)
    lane = lax.broadcasted_iota(jnp.int32, (GROUP, LANES), 1)

    def conv_silu(ref, cw_ref, r0, g, lanes):
        cur = ref[0, pl.ds(r0, GROUP), lanes].astype(F32)
        pstart = pl.multiple_of(jnp.maximum(r0 - 16, 0), 16)
        prev = ref[0, pl.ds(pstart, 16), lanes].astype(F32)[8:16]
        prev = jnp.where(g > 0, prev, 0.0)
        w = cw_ref[:, lanes]
        acc = cur * w[GDN_CONV - 1:GDN_CONV]
        for s in range(1, GDN_CONV):
            acc = acc + _shift_rows(cur, prev, s) * w[GDN_CONV - 1 - s:GDN_CONV - s]
        return _silu(acc)

    def l2n(x):
        return x * lax.rsqrt(jnp.sum(x * x, axis=-1, keepdims=True) + EPS)

    def group_step(g, carry):
        r0 = pl.multiple_of(g * GROUP, GROUP)
        blk = bgc_ref[0, pl.ds(r0, GROUP), :]
        rows = pl.ds(r0, GROUP)

        def prep(hh):
            lanes = slice(hh * LANES, (hh + 1) * LANES)
            head = hbase + hh
            qn = l2n(conv_silu(q_ref, cwq_ref, r0, g, lanes)) * (GDN_D ** -0.5)
            kn = l2n(conv_silu(k_ref, cwk_ref, r0, g, lanes))
            vn = conv_silu(v_ref, cwv_ref, r0, g, lanes)

            def col(off):
                return jnp.sum(jnp.where(lane == head + off, blk, 0.0), axis=-1, keepdims=True)

            beta = col(0)
            gc = col(GDN_HEADS)
            glast = col(2 * GDN_HEADS)
            gc_row = gt_ref[0, pl.ds(head, 1), pl.ds(r0, GROUP)]
            eg = jnp.exp(gc)
            kb = kn * beta
            rhs = jnp.concatenate([vn * beta, kb * eg], axis=1).astype(BF16)
            q_dec = qn * eg
            k_dec = (kn * jnp.exp(glast - gc)).astype(BF16)
            kn_b = kn.astype(BF16)
            kk = _mm_nt(kb, kn_b)
            qk = _mm_nt(qn, kn_b)
            decay = jnp.exp(jnp.where(same64 & tri, gc - gc_row, -1e30))
            lmat = jnp.where(strict, kk * decay, 0.0)
            intra = (qk * decay).astype(BF16)
            return lmat, intra, rhs, q_dec, k_dec, glast

        pre = [prep(hh) for hh in range(hb)]
        hs = range(hb)
        lm = [p[0] for p in pre]

        d8 = [jnp.where(same8, lm[h], 0.0).astype(BF16) for h in hs]
        d8_2 = [_mm(d8[h], d8[h]).astype(BF16) for h in hs]
        d8_4 = [_mm(d8_2[h], d8_2[h]) for h in hs]
        x = [eye - d8[h].astype(F32) for h in hs]
        x = [x[h] + _mm(x[h], d8_2[h]) for h in hs]
        x = [x[h] + _mm(x[h], d8_4[h]) for h in hs]
        for inner, outer in ((same8, same16), (same16, same32), (same32, same64)):
            sel = outer & jnp.logical_not(inner)
            xb = [x[h].astype(BF16) for h in hs]
            t = [_mm(jnp.where(sel, lm[h], 0.0), xb[h]) for h in hs]
            x = [x[h] - _mm(xb[h], t[h]) for h in hs]

        uw = [_mm(x[h], pre[h][2]).astype(BF16) for h in hs]
        iu = [_mm(pre[h][1], uw[h]) for h in hs]
        for h in hs:
            _, _, _, q_dec, k_dec, glast = pre[h]
            o0_scr[h, rows, :] = iu[h][:, :GDN_D]
            qe_scr[h, rows, :] = (q_dec - iu[h][:, GDN_D:]).astype(BF16)
            for c in range(cpg):
                cs = slice(c * CHUNK, (c + 1) * CHUNK)
                bm = _mm_tn(k_dec[cs], uw[h][cs])
                n = g * cpg + c
                b_scr[h, n] = bm[:, :GDN_D]
                m_scr[h, n] = bm[:, GDN_D:].astype(BF16)
                dl_scr[h, n] = jnp.broadcast_to(
                    jnp.exp(glast[c * CHUNK:c * CHUNK + 8]), (8, LANES))
        return carry

    lax.fori_loop(0, ngroups, group_step, 0)

    nw = nw_ref[...]

    def chunk_step(n, states):
        rows = pl.ds(pl.multiple_of(n * CHUNK, CHUNK), CHUNK)
        new_states = []
        for hh in range(hb):
            lanes = slice(hh * LANES, (hh + 1) * LANES)
            s = states[hh]
            sb = s.astype(BF16)
            o = jnp.dot(qe_scr[hh, rows, :], sb, preferred_element_type=F32) + o0_scr[hh, rows, :]
            ms = jnp.mean(o * o, axis=-1, keepdims=True)
            zz = z_ref[0, rows, lanes].astype(F32)
            o_ref[0, rows, lanes] = (o * lax.rsqrt(ms + EPS) * nw * _silu(zz)).astype(BF16)
            s = (s * dl_scr[hh, n][0:1, :]
                 - jnp.dot(m_scr[hh, n], sb, preferred_element_type=F32) + b_scr[hh, n])
            new_states.append(s)
        return tuple(new_states)

    init = tuple(jnp.zeros((GDN_D, GDN_D), F32) for _ in range(hb))
    lax.fori_loop(0, nchunks, chunk_step, init)


def _gdn(u3, conv_w, bgc3, gt, norm_w, hb):
    b, seq, _ = u3.shape
    w = hb * LANES
    nblk = GDN_HEADS // hb
    nchunks = seq // CHUNK
    body = functools.partial(_gdn_body, hb=hb, seq=seq)

    def tok(col0):
        return pl.BlockSpec((1, seq, w), lambda i, j: (i, 0, col0 + j))

    def cw(col0):
        return pl.BlockSpec((GDN_CONV, w), lambda i, j: (0, col0 + j))

    return pl.pallas_call(
        body,
        grid=(b, nblk),
        in_specs=[
            tok(0), tok(nblk), tok(2 * nblk), tok(3 * nblk),
            cw(0), cw(nblk), cw(2 * nblk),
            pl.BlockSpec((1, seq, LANES), lambda i, j: (i, 0, 0)),
            pl.BlockSpec((1, GDN_HEADS, seq), lambda i, j: (i, 0, 0)),
            pl.BlockSpec((1, GDN_D), lambda i, j: (0, 0)),
        ],
        out_specs=pl.BlockSpec((1, seq, w), lambda i, j: (i, 0, j)),
        out_shape=jax.ShapeDtypeStruct((b, seq, GDN_HEADS * GDN_D), BF16),
        scratch_shapes=[
            pltpu.VMEM((hb, nchunks, GDN_D, GDN_D), BF16),
            pltpu.VMEM((hb, nchunks, GDN_D, GDN_D), F32),
            pltpu.VMEM((hb, seq, GDN_D), BF16),
            pltpu.VMEM((hb, seq, GDN_D), F32),
            pltpu.VMEM((hb, nchunks, 8, LANES), F32),
        ],
        compiler_params=pltpu.CompilerParams(
            dimension_semantics=("arbitrary", "arbitrary"), vmem_limit_bytes=VMEM_LIMIT),
        name="gdn",
    )(u3, u3, u3, u3, conv_w, conv_w, conv_w, bgc3, gt, norm_w)


def _attn_body(q_ref, k_ref, v_ref, c_ref, sa_ref, sb_ref, qnw_ref, knw_ref, lam_ref, sw_ref,
               o_ref, k_scr, q_scr, vt_scr, *, seq, ta, lambda_init):
    nt = seq // ta
    low = lax.broadcasted_iota(jnp.int32, (ta, LANES), 1) < DIFF_D

    def norm_rope(x, w, rows):
        x2 = x * x
        lo = jnp.sum(jnp.where(low, x2, 0.0), axis=-1, keepdims=True)
        hi = jnp.sum(jnp.where(low, 0.0, x2), axis=-1, keepdims=True)
        ms = jnp.where(low, lo, hi) * (1.0 / DIFF_D)
        y = x * lax.rsqrt(ms + EPS) * w
        half = ROPE_DIM // 2
        return (y * c_ref[rows, :] + pltpu.roll(y, LANES - half, axis=1) * sa_ref[rows, :]
                + pltpu.roll(y, half, axis=1) * sb_ref[rows, :])

    qscale = (DIFF_D ** -0.5) * math.log2(math.e)
    ri = lax.broadcasted_iota(jnp.int32, (LANES, LANES), 0)
    ci = lax.broadcasted_iota(jnp.int32, (LANES, LANES), 1)
    eye = (ri == ci).astype(BF16)
    nt_dims = (((1,), (1,)), ((), ()))
    for t in range(nt):
        rows = slice(t * ta, (t + 1) * ta)
        k_scr[rows, :] = norm_rope(k_ref[0, rows, :].astype(F32), knw_ref[...], rows).astype(BF16)
        qx = norm_rope(q_ref[0, rows, :].astype(F32), qnw_ref[...], rows) * qscale
        q_scr[0, rows, :] = jnp.where(low, qx, 0.0).astype(BF16)
        q_scr[1, rows, :] = jnp.where(low, 0.0, qx).astype(BF16)
        vt_scr[0:LANES, rows] = lax.dot_general(eye, v_ref[0, rows, :], nt_dims,
                                                preferred_element_type=F32).astype(BF16)
    vt_scr[LANES:, :] = jnp.ones((ONES_ROWS, seq), BF16)

    lam = (jnp.exp(jnp.sum(lam_ref[0:1, :] * lam_ref[1:2, :], axis=-1, keepdims=True))
           - jnp.exp(jnp.sum(lam_ref[2:3, :] * lam_ref[3:4, :], axis=-1, keepdims=True))
           + lambda_init)
    krow = lax.broadcasted_iota(jnp.int32, (ta, ta), 0)
    qcol = lax.broadcasted_iota(jnp.int32, (ta, ta), 1)
    causal = krow <= qcol

    def scores(qi):
        d0 = qi * ta
        out = []
        for mp in range(2):
            qz = q_scr[mp, d0:d0 + ta, :]
            diag = lax.dot_general(k_scr[d0:d0 + ta, :], qz, nt_dims, preferred_element_type=F32)
            diag = jnp.where(causal, diag, -1e30)
            past = None
            if qi > 0:
                past = lax.dot_general(k_scr[0:d0, :], qz, nt_dims, preferred_element_type=F32)
            out.append((past, diag))
        return out

    def finish(qi, sc):
        d0 = qi * ta
        outs = []
        for past, diag in sc:
            m = jnp.max(diag, axis=0, keepdims=True)
            if past is not None:
                m = jnp.maximum(m, jnp.max(past, axis=0, keepdims=True))
            pd = jnp.exp2(diag - m)
            acc = jnp.dot(vt_scr[:, d0:d0 + ta], pd.astype(BF16), preferred_element_type=F32)
            if past is not None:
                pp = jnp.exp2(past - m)
                acc = acc + jnp.dot(vt_scr[:, 0:d0], pp.astype(BF16), preferred_element_type=F32)
            outs.append(acc[0:LANES] / acc[LANES:LANES + 1])
        o = (outs[0] - lam * outs[1]).T
        ms = jnp.mean(o * o, axis=-1, keepdims=True)
        o_ref[0, d0:d0 + ta, :] = (o * lax.rsqrt(ms + EPS) * sw_ref[...]
                                   * (1.0 - lambda_init)).astype(BF16)

    sc = scores(0)
    for qi in range(nt):
        nxt = scores(qi + 1) if qi + 1 < nt else None
        finish(qi, sc)
        sc = nxt


def _attn(u3, rope_tabs, qnw, knw, lam4, subln_w, lambda_init, ta):
    b, seq, _ = u3.shape
    c_t, sa_t, sb_t = rope_tabs
    nt = seq // ta
    body = functools.partial(_attn_body, seq=seq, ta=ta, lambda_init=lambda_init)
    qcol, kcol, vcol = 4 * GDN_HEADS, 4 * GDN_HEADS + DIFF_HEADS, 4 * GDN_HEADS + 2 * DIFF_HEADS
    tab = pl.BlockSpec((seq, LANES), lambda i, h: (0, 0))
    vec = pl.BlockSpec((1, LANES), lambda i, h: (0, 0))

    def tok(col0):
        return pl.BlockSpec((1, seq, LANES), lambda i, h: (i, 0, col0 + h))

    return pl.pallas_call(
        body,
        grid=(b, DIFF_HEADS),
        in_specs=[tok(qcol), tok(kcol), tok(vcol), tab, tab, tab, vec, vec,
                  pl.BlockSpec((4, LANES), lambda i, h: (0, 0)), vec],
        out_specs=pl.BlockSpec((1, seq, LANES), lambda i, h: (i, 0, h)),
        out_shape=jax.ShapeDtypeStruct((b, seq, DIFF_HEADS * 2 * DIFF_D), BF16),
        scratch_shapes=[
            pltpu.VMEM((seq, LANES), BF16),
            pltpu.VMEM((2, seq, LANES), BF16),
            pltpu.VMEM((LANES + ONES_ROWS, seq), BF16),
        ],
        compiler_params=pltpu.CompilerParams(
            dimension_semantics=("arbitrary", "arbitrary"), vmem_limit_bytes=VMEM_LIMIT),
        name="attn",
    )(u3, u3, u3, c_t, sa_t, sb_t, qnw, knw, lam4, subln_w)


def _mixer_body(x_ref, og_ref, od_ref, ga_ref, gb_ref, bg_ref, wg_ref, wd_ref, wo_ref, o_ref):
    d = x_ref.shape[1]
    ya = jnp.dot(og_ref[...], wg_ref[...], preferred_element_type=F32)
    yb = jnp.dot(od_ref[...], wd_ref[...], preferred_element_type=F32)
    ga = _sigmoid(ga_ref[...].astype(F32) + bg_ref[:, :d])
    gb = _sigmoid(gb_ref[...].astype(F32) + bg_ref[:, d:])
    mix = (ga * ya + gb * yb).astype(BF16)
    o_ref[...] = x_ref[...] + jnp.dot(mix, wo_ref[...], preferred_element_type=F32)


def _mixer(x2d, og, od, u, b_gate, wg, wd, wo, gate_col, tm):
    m, d = x2d.shape
    row = lambda i: (i, 0)
    full = lambda i: (0, 0)
    return pl.pallas_call(
        _mixer_body,
        grid=(m // tm,),
        in_specs=[
            pl.BlockSpec((tm, d), row),
            pl.BlockSpec((tm, d), row),
            pl.BlockSpec((tm, d), row),
            pl.BlockSpec((tm, d), lambda i: (i, gate_col)),
            pl.BlockSpec((tm, d), lambda i: (i, gate_col + 1)),
            pl.BlockSpec((1, 2 * d), full),
            pl.BlockSpec((d, d), full),
            pl.BlockSpec((d, d), full),
            pl.BlockSpec((d, d), full),
        ],
        out_specs=pl.BlockSpec((tm, d), row),
        out_shape=jax.ShapeDtypeStruct((m, d), F32),
        compiler_params=pltpu.CompilerParams(
            dimension_semantics=("arbitrary",), vmem_limit_bytes=VMEM_LIMIT),
        name="mixer",
    )(x2d, og, od, u, u, b_gate, wg, wd, wo)


def _ffn_body(x_ref, n2_ref, wup_ref, cw_ref, cb_ref, wdn_ref, o_ref, carry_scr, *, tpb, tf):
    i = pl.program_id(0)
    dff = wdn_ref.shape[0]
    x = x_ref[...]
    ms = jnp.mean(x * x, axis=-1, keepdims=True)
    hb = (x * lax.rsqrt(ms + EPS) * n2_ref[...]).astype(BF16)
    tm = x.shape[0]
    first = (i % tpb) == 0

    def conv(u, cols):
        prev = jnp.where(first, 0.0, carry_scr[:, cols])
        w = cw_ref[:, cols]
        acc = u * w[FFN_CONV - 1:FFN_CONV] + cb_ref[:, cols]
        for s in range(1, FFN_CONV):
            acc = acc + _shift_rows(u, prev, s) * w[FFN_CONV - 1 - s:FFN_CONV - s]
        carry_scr[:, cols] = u[tm - 8:tm]
        return acc

    acc = x
    for f in range(dff // tf):
        gcols = slice(f * tf, (f + 1) * tf)
        vcols = slice(dff + f * tf, dff + (f + 1) * tf)
        ug = jnp.dot(hb, wup_ref[:, gcols], preferred_element_type=F32)
        uv = jnp.dot(hb, wup_ref[:, vcols], preferred_element_type=F32)
        act = (_silu(conv(ug, gcols)) * conv(uv, vcols)).astype(BF16)
        acc = acc + jnp.dot(act, wdn_ref[gcols, :], preferred_element_type=F32)
    o_ref[...] = acc


def _ffn(x2d, n2, wup, cw, cb, wdn, seq, tm, tf):
    m, d = x2d.shape
    dff = wdn.shape[0]
    full = lambda i: (0, 0)
    body = functools.partial(_ffn_body, tpb=seq // tm, tf=tf)
    return pl.pallas_call(
        body,
        grid=(m // tm,),
        in_specs=[
            pl.BlockSpec((tm, d), lambda i: (i, 0)),
            pl.BlockSpec((1, d), full),
            pl.BlockSpec((d, 2 * dff), full, pipeline_mode=pl.Buffered(1)),
            pl.BlockSpec((FFN_CONV, 2 * dff), full),
            pl.BlockSpec((1, 2 * dff), full),
            pl.BlockSpec((dff, d), full, pipeline_mode=pl.Buffered(1)),
        ],
        out_specs=pl.BlockSpec((tm, d), lambda i: (i, 0)),
        out_shape=jax.ShapeDtypeStruct((m, d), F32),
        scratch_shapes=[pltpu.VMEM((8, 2 * dff), F32)],
        compiler_params=pltpu.CompilerParams(
            dimension_semantics=("arbitrary",), vmem_limit_bytes=VMEM_LIMIT),
        name="ffn",
    )(x2d, n2, wup, cw, cb, wdn)


def _rope_tables(seq):
    half = ROPE_DIM // 2
    pos = jnp.arange(seq, dtype=F32)
    inv_freq = ROPE_THETA ** (-jnp.arange(0, ROPE_DIM, 2, dtype=F32) / ROPE_DIM)
    ang = pos[:, None] * inv_freq[None, :]
    cos, sin = jnp.cos(ang), jnp.sin(ang)
    ones = jnp.ones((seq, DIFF_D - ROPE_DIM), F32)
    zeros = jnp.zeros((seq, DIFF_D - ROPE_DIM), F32)
    zh = jnp.zeros((seq, half), F32)
    c = jnp.concatenate([cos, cos, ones], axis=-1)
    sa = jnp.concatenate([-sin, zh, zeros], axis=-1)
    sb = jnp.concatenate([zh, sin, zeros], axis=-1)
    return tuple(jnp.concatenate([t, t], axis=-1) for t in (c, sa, sb))


def _pad_lanes(v, offset):
    return jnp.zeros((LANES,), F32).at[offset:offset + v.shape[0]].set(v.astype(F32))


def kernel(x, norm1_w, w_in, b_gate, gdn_conv_w, gdn_A_log, gdn_dt_bias, gdn_norm_w,
           diff_q_norm_w, diff_k_norm_w, lambda_q1, lambda_k1, lambda_q2, lambda_k2,
           diff_subln_w, w_gdn_out, w_diff_out, w_o, norm2_w, w_up, ffn_conv_w,
           ffn_conv_b, w_down):
    b, seq, d = x.shape
    depth = norm1_w.shape[0]
    gqk = GDN_HEADS * GDN_D
    dqk = DIFF_HEADS * 2 * DIFF_D
    o_b = 4 * gqk
    o_dq = o_b + 2 * GDN_HEADS
    o_gate = o_dq + 3 * dqk
    dff = w_down.shape[1]
    rope_tabs = _rope_tables(seq)
    x2d = x.reshape(b * seq, d)
    tm_in = min(1024, seq)
    tq = min(256, seq)
    tm_mix = min(512, seq)
    tm_ffn = min(512, seq)
    tf = dff // 2

    for layer in range(depth):
        lambda_init = 0.8 - 0.6 * math.exp(-0.3 * layer)
        wl = w_in[layer]
        w_main = jnp.concatenate([wl[:, :o_b], wl[:, o_dq:]], axis=1).astype(BF16)
        w_ba = jnp.pad(wl[:, o_b:o_dq], ((0, 0), (0, LANES - 2 * GDN_HEADS))).astype(BF16)
        gp = jnp.stack([_pad_lanes(gdn_A_log[layer], GDN_HEADS),
                        _pad_lanes(gdn_dt_bias[layer], GDN_HEADS)])
        u, bgc, gt = _inproj(x2d, norm1_w[layer][None, :], w_main, w_ba, gp, seq, tm_in, 1024)
        u3 = u.reshape(b, seq, u.shape[1])
        og = _gdn(u3, gdn_conv_w[layer], bgc.reshape(b, seq, LANES), gt,
                  gdn_norm_w[layer][None, :], hb=4)
        qnw = jnp.tile(diff_q_norm_w[layer], 2)[None, :]
        knw = jnp.tile(diff_k_norm_w[layer], 2)[None, :]
        lam4 = jnp.stack([jnp.pad(v[layer], (0, LANES - DIFF_D))
                          for v in (lambda_q1, lambda_k1, lambda_q2, lambda_k2)])
        od = _attn(u3, rope_tabs, qnw, knw, lam4, diff_subln_w[layer][None, :], lambda_init, tq)
        x2d = _mixer(x2d, og.reshape(b * seq, gqk), od.reshape(b * seq, dqk), u,
                     b_gate[layer][None, :], w_gdn_out[layer].astype(BF16),
                     w_diff_out[layer].astype(BF16), w_o[layer].astype(BF16),
                     (4 * gqk + 3 * dqk) // d, tm_mix)
        x2d = _ffn(x2d, norm2_w[layer][None, :], w_up[layer].astype(BF16), ffn_conv_w[layer],
                   ffn_conv_b[layer][None, :], w_down[layer].astype(BF16), seq, tm_ffn, tf)
    return x2d.reshape(b, seq, d)
```

```python
import functools
import math

import jax
import jax.numpy as jnp
from jax import lax
from jax.experimental import pallas as pl
from jax.experimental.pallas import tpu as pltpu

F32 = jnp.float32
BF16 = jnp.bfloat16

EPS = 1e-6
LANES = 128
GDN_HEADS = 8
GDN_D = 128
GDN_CONV = 4
CHUNK = 64
GROUP = 256
DIFF_HEADS = 8
DIFF_D = 64
ROPE_DIM = DIFF_D // 4
ROPE_THETA = 500000.0
FFN_CONV = 3
ONES_ROWS = 16
BOUND_SLACK = 1.01
MAX_SAFE_SHIFT = 50.0
VMEM_LIMIT = 56 * 1024 * 1024


def _mm(a, b):
    return jnp.dot(a.astype(BF16), b.astype(BF16), preferred_element_type=F32)


def _mm_nt(a, b):
    return lax.dot_general(a.astype(BF16), b.astype(BF16), (((1,), (1,)), ((), ())),
                           preferred_element_type=F32)


def _mm_tn(a, b):
    return lax.dot_general(a.astype(BF16), b.astype(BF16), (((0,), (0,)), ((), ())),
                           preferred_element_type=F32)


def _sigmoid(x):
    return 1.0 / (1.0 + jnp.exp(-x))


def _silu(x):
    return x * _sigmoid(x)


def _shift_rows(cur, prev8, s):
    rc = pltpu.roll(cur, s, axis=0)
    rp = pltpu.roll(prev8, s, axis=0)
    row = lax.broadcasted_iota(jnp.int32, prev8.shape, 0)
    first = jnp.where(row < s, rp, rc[0:8])
    return jnp.concatenate([first, rc[8:]], axis=0)


def _inproj_body(x_ref, n1_ref, w_ref, wba_ref, gp_ref, u_ref, bgc_ref, gt_ref, h_scr):
    j = pl.program_id(1)

    @pl.when(j == 0)
    def _():
        x = x_ref[...]
        ms = jnp.mean(x * x, axis=-1, keepdims=True)
        hb = (x * lax.rsqrt(ms + EPS) * n1_ref[...]).astype(BF16)
        h_scr[...] = hb
        ba = jnp.dot(hb, wba_ref[...], preferred_element_type=F32)
        tm = ba.shape[0]
        lane = lax.broadcasted_iota(jnp.int32, ba.shape, 1)
        row = lax.broadcasted_iota(jnp.int32, ba.shape, 0)
        beta = _sigmoid(ba)
        zz = ba + gp_ref[1:2, :]
        softplus = jnp.maximum(zz, 0.0) + jnp.log1p(jnp.exp(-jnp.abs(zz)))
        g = -jnp.exp(gp_ref[0:1, :]) * softplus
        rin = row & (CHUNK - 1)
        gc = g
        s = 1
        while s < CHUNK:
            gc = gc + jnp.where(rin >= s, pltpu.roll(gc, s, axis=0), 0.0)
            s *= 2
        glast = jnp.broadcast_to(
            gc.reshape(tm // CHUNK, CHUNK, LANES)[:, CHUNK - 1:CHUNK, :],
            (tm // CHUNK, CHUNK, LANES)).reshape(tm, LANES)
        glast = pltpu.roll(glast, GDN_HEADS, axis=1)
        out = jnp.where(lane < GDN_HEADS, beta,
                        jnp.where(lane < 2 * GDN_HEADS, gc,
                                  jnp.where(lane < 3 * GDN_HEADS, glast, 0.0)))
        bgc_ref[...] = out
        gt_ref[0] = out.T[GDN_HEADS:2 * GDN_HEADS, :]

    u_ref[...] = jnp.dot(h_scr[...], w_ref[...], preferred_element_type=F32).astype(BF16)


def _inproj(x2d, n1, w, wba, gp, seq, tm, tn):
    m, d = x2d.shape
    n = w.shape[1]
    tpb = seq // tm
    return pl.pallas_call(
        _inproj_body,
        grid=(m // tm, n // tn),
        in_specs=[
            pl.BlockSpec((tm, d), lambda i, j: (i, 0)),
            pl.BlockSpec((1, d), lambda i, j: (0, 0)),
            pl.BlockSpec((d, tn), lambda i, j: (0, j)),
            pl.BlockSpec((d, LANES), lambda i, j: (0, 0)),
            pl.BlockSpec((2, LANES), lambda i, j: (0, 0)),
        ],
        out_specs=[
            pl.BlockSpec((tm, tn), lambda i, j: (i, j)),
            pl.BlockSpec((tm, LANES), lambda i, j: (i, 0)),
            pl.BlockSpec((1, GDN_HEADS, tm), lambda i, j: (i // tpb, 0, i % tpb)),
        ],
        out_shape=[
            jax.ShapeDtypeStruct((m, n), BF16),
            jax.ShapeDtypeStruct((m, LANES), F32),
            jax.ShapeDtypeStruct((m // seq, GDN_HEADS, seq), F32),
        ],
        scratch_shapes=[pltpu.VMEM((tm, d), BF16)],
        compiler_params=pltpu.CompilerParams(
            dimension_semantics=("arbitrary", "arbitrary"), vmem_limit_bytes=VMEM_LIMIT),
        name="inproj",
    )(x2d, n1, w, wba, gp)


def _gdn_body(q_ref, k_ref, v_ref, z_ref, cwq_ref, cwk_ref, cwv_ref, bgc_ref, gt_ref, nw_ref,
              o_ref, m_scr, b_scr, qe_scr, o0_scr, dl_scr, *, hb, seq):
    hbase = pl.program_id(1) * hb
    ngroups = seq // GROUP
    nchunks = seq // CHUNK
    cpg = GROUP // CHUNK

    ri = lax.broadcasted_iota(jnp.int32, (GROUP, GROUP), 0)
    ci = lax.broadcasted_iota(jnp.int32, (GROUP, GROUP), 1)
    tri = ci <= ri
    strict = ci < ri
    same8 = (ri >> 3) == (ci >> 3)
    same16 = (ri >> 4) == (ci >> 4)
    same32 = (ri >> 5) == (ci >> 5)
    same64 = (ri >> 6) == (ci >> 6)
    eye = (ri == ci).astype(F32)
    lane = lax.broadcasted_iota(jnp.int32, (GROUP, LANES), 1)

    def conv_silu(ref, cw_ref, r0, g, lanes):
        cur = ref[0, pl.ds(r0, GROUP), lanes].astype(F32)
        pstart = pl.multiple_of(jnp.maximum(r0 - 16, 0), 16)
        prev = ref[0, pl.ds(pstart, 16), lanes].astype(F32)[8:16]
        prev = jnp.where(g > 0, prev, 0.0)
        w = cw_ref[:, lanes]
        acc = cur * w[GDN_CONV - 1:GDN_CONV]
        for s in range(1, GDN_CONV):
            acc = acc + _shift_rows(cur, prev, s) * w[GDN_CONV - 1 - s:GDN_CONV - s]
        return _silu(acc)

    def l2n(x):
        return x * lax.rsqrt(jnp.sum(x * x, axis=-1, keepdims=True) + EPS)

    def group_step(g, carry):
        r0 = pl.multiple_of(g * GROUP, GROUP)
        blk = bgc_ref[0, pl.ds(r0, GROUP), :]
        rows = pl.ds(r0, GROUP)

        def prep(hh):
            lanes = slice(hh * LANES, (hh + 1) * LANES)
            head = hbase + hh
            qn = l2n(conv_silu(q_ref, cwq_ref, r0, g, lanes)) * (GDN_D ** -0.5)
            kn = l2n(conv_silu(k_ref, cwk_ref, r0, g, lanes))
            vn = conv_silu(v_ref, cwv_ref, r0, g, lanes)

            def col(off):
                return jnp.sum(jnp.where(lane == head + off, blk, 0.0), axis=-1, keepdims=True)

            beta = col(0)
            gc = col(GDN_HEADS)
            glast = col(2 * GDN_HEADS)
            gc_row = gt_ref[0, pl.ds(head, 1), pl.ds(r0, GROUP)]
            eg = jnp.exp(gc)
            kb = kn * beta
            rhs = jnp.concatenate([vn * beta, kb * eg], axis=1).astype(BF16)
            q_dec = qn * eg
            k_dec = (kn * jnp.exp(glast - gc)).astype(BF16)
            kn_b = kn.astype(BF16)
            kk = _mm_nt(kb, kn_b)
            qk = _mm_nt(qn, kn_b)
            decay = jnp.exp(jnp.where(same64 & tri, gc - gc_row, -1e30))
            lmat = jnp.where(strict, kk * decay, 0.0)
            intra = (qk * decay).astype(BF16)
            return lmat, intra, rhs, q_dec, k_dec, glast

        pre = [prep(hh) for hh in range(hb)]
        hs = range(hb)
        lm = [p[0] for p in pre]

        d8 = [jnp.where(same8, lm[h], 0.0).astype(BF16) for h in hs]
        d8_2 = [_mm(d8[h], d8[h]).astype(BF16) for h in hs]
        d8_4 = [_mm(d8_2[h], d8_2[h]) for h in hs]
        x = [eye - d8[h].astype(F32) for h in hs]
        x = [x[h] + _mm(x[h], d8_2[h]) for h in hs]
        x = [x[h] + _mm(x[h], d8_4[h]) for h in hs]
        for inner, outer in ((same8, same16), (same16, same32), (same32, same64)):
            sel = outer & jnp.logical_not(inner)
            xb = [x[h].astype(BF16) for h in hs]
            t = [_mm(jnp.where(sel, lm[h], 0.0), xb[h]) for h in hs]
            x = [x[h] - _mm(xb[h], t[h]) for h in hs]

        uw = [_mm(x[h], pre[h][2]).astype(BF16) for h in hs]
        iu = [_mm(pre[h][1], uw[h]) for h in hs]
        for h in hs:
            _, _, _, q_dec, k_dec, glast = pre[h]
            o0_scr[h, rows, :] = iu[h][:, :GDN_D]
            qe_scr[h, rows, :] = (q_dec - iu[h][:, GDN_D:]).astype(BF16)
            for c in range(cpg):
                cs = slice(c * CHUNK, (c + 1) * CHUNK)
                bm = _mm_tn(k_dec[cs], uw[h][cs])
                n = g * cpg + c
                b_scr[h, n] = bm[:, :GDN_D]
                m_scr[h, n] = bm[:, GDN_D:].astype(BF16)
                dl_scr[h, n] = jnp.broadcast_to(
                    jnp.exp(glast[c * CHUNK:c * CHUNK + 8]), (8, LANES))
        return carry

    lax.fori_loop(0, ngroups, group_step, 0)

    nw = nw_ref[...]

    def chunk_step(n, states):
        rows = pl.ds(pl.multiple_of(n * CHUNK, CHUNK), CHUNK)
        new_states = []
        for hh in range(hb):
            lanes = slice(hh * LANES, (hh + 1) * LANES)
            s = states[hh]
            sb = s.astype(BF16)
            o = jnp.dot(qe_scr[hh, rows, :], sb, preferred_element_type=F32) + o0_scr[hh, rows, :]
            ms = jnp.mean(o * o, axis=-1, keepdims=True)
            zz = z_ref[0, rows, lanes].astype(F32)
            o_ref[0, rows, lanes] = (o * lax.rsqrt(ms + EPS) * nw * _silu(zz)).astype(BF16)
            s = (s * dl_scr[hh, n][0:1, :]
                 - jnp.dot(m_scr[hh, n], sb, preferred_element_type=F32) + b_scr[hh, n])
            new_states.append(s)
        return tuple(new_states)

    init = tuple(jnp.zeros((GDN_D, GDN_D), F32) for _ in range(hb))
    lax.fori_loop(0, nchunks, chunk_step, init)


def _gdn(u3, conv_w, bgc3, gt, norm_w, hb):
    b, seq, _ = u3.shape
    w = hb * LANES
    nblk = GDN_HEADS // hb
    nchunks = seq // CHUNK
    body = functools.partial(_gdn_body, hb=hb, seq=seq)

    def tok(col0):
        return pl.BlockSpec((1, seq, w), lambda i, j: (i, 0, col0 + j))

    def cw(col0):
        return pl.BlockSpec((GDN_CONV, w), lambda i, j: (0, col0 + j))

    return pl.pallas_call(
        body,
        grid=(b, nblk),
        in_specs=[
            tok(0), tok(nblk), tok(2 * nblk), tok(3 * nblk),
            cw(0), cw(nblk), cw(2 * nblk),
            pl.BlockSpec((1, seq, LANES), lambda i, j: (i, 0, 0)),
            pl.BlockSpec((1, GDN_HEADS, seq), lambda i, j: (i, 0, 0)),
            pl.BlockSpec((1, GDN_D), lambda i, j: (0, 0)),
        ],
        out_specs=pl.BlockSpec((1, seq, w), lambda i, j: (i, 0, j)),
        out_shape=jax.ShapeDtypeStruct((b, seq, GDN_HEADS * GDN_D), BF16),
        scratch_shapes=[
            pltpu.VMEM((hb, nchunks, GDN_D, GDN_D), BF16),
            pltpu.VMEM((hb, nchunks, GDN_D, GDN_D), F32),
            pltpu.VMEM((hb, seq, GDN_D), BF16),
            pltpu.VMEM((hb, seq, GDN_D), F32),
            pltpu.VMEM((hb, nchunks, 8, LANES), F32),
        ],
        compiler_params=pltpu.CompilerParams(
            dimension_semantics=("arbitrary", "arbitrary"), vmem_limit_bytes=VMEM_LIMIT),
        name="gdn",
    )(u3, u3, u3, u3, conv_w, conv_w, conv_w, bgc3, gt, norm_w)


def _attn_body(q_ref, k_ref, v_ref, c_ref, sa_ref, sb_ref, qnw_ref, knw_ref, lam_ref, sw_ref,
               o_ref, k_scr, q_scr, vt_scr, *, seq, ta, lambda_init):
    nt = seq // ta
    low = lax.broadcasted_iota(jnp.int32, (ta, LANES), 1) < DIFF_D

    lane = lax.broadcasted_iota(jnp.int32, (ta, LANES), 1)
    ri = lax.broadcasted_iota(jnp.int32, (LANES, LANES), 0)
    ci = lax.broadcasted_iota(jnp.int32, (LANES, LANES), 1)
    eye = (ri == ci).astype(BF16)
    same_half = ((ri < DIFF_D) == (ci < DIFF_D)).astype(BF16)
    nt_dims = (((1,), (1,)), ((), ()))

    def norm_rope(x, w, rows):
        ms = jnp.dot((x * x).astype(BF16), same_half, preferred_element_type=F32) * (1.0 / DIFF_D)
        y = x * lax.rsqrt(ms + EPS) * w
        half = ROPE_DIM // 2
        return (y * c_ref[rows, :] + pltpu.roll(y, LANES - half, axis=1) * sa_ref[rows, :]
                + pltpu.roll(y, half, axis=1) * sb_ref[rows, :])

    qscale = (DIFF_D ** -0.5) * math.log2(math.e)
    wq_max = jnp.max(jnp.abs(qnw_ref[...]), axis=-1, keepdims=True)
    wk_max = jnp.max(jnp.abs(knw_ref[...]), axis=-1, keepdims=True)
    shift = (BOUND_SLACK * DIFF_D * qscale) * wq_max * wk_max
    bound_ok = jnp.max(shift) <= MAX_SAFE_SHIFT
    s_hi = shift.astype(BF16).astype(F32)
    s_mid = (shift - s_hi).astype(BF16).astype(F32)
    s_lo = shift - s_hi - s_mid
    shift_lane = (DIFF_D, 0)

    def shift_terms(base):
        return jnp.where(lane == base, -s_hi,
                         jnp.where(lane == base + 1, -s_mid, jnp.where(lane == base + 2, -s_lo, 0.0)))

    def ones_lanes(base):
        return jnp.where((lane >= base) & (lane < base + 3), 1.0, 0.0)

    def prepare():
        for t in range(nt):
            rows = slice(t * ta, (t + 1) * ta)
            kx = norm_rope(k_ref[0, rows, :].astype(F32), knw_ref[...], rows)
            k_scr[0, rows, :] = jnp.where(low, kx, ones_lanes(shift_lane[0])).astype(BF16)
            k_scr[1, rows, :] = jnp.where(low, ones_lanes(shift_lane[1]), kx).astype(BF16)
            qx = norm_rope(q_ref[0, rows, :].astype(F32), qnw_ref[...], rows) * qscale
            q_scr[0, rows, :] = jnp.where(low, qx, shift_terms(shift_lane[0])).astype(BF16)
            q_scr[1, rows, :] = jnp.where(low, shift_terms(shift_lane[1]), qx).astype(BF16)
            vt_scr[0:LANES, rows] = lax.dot_general(eye, v_ref[0, rows, :], nt_dims,
                                                    preferred_element_type=F32).astype(BF16)
        vt_scr[LANES:, :] = jnp.ones((ONES_ROWS, seq), BF16)

    lam = (jnp.exp(jnp.sum(lam_ref[0:1, :] * lam_ref[1:2, :], axis=-1, keepdims=True))
           - jnp.exp(jnp.sum(lam_ref[2:3, :] * lam_ref[3:4, :], axis=-1, keepdims=True))
           + lambda_init)
    krow = lax.broadcasted_iota(jnp.int32, (ta, ta), 0)
    qcol = lax.broadcasted_iota(jnp.int32, (ta, ta), 1)
    causal = krow <= qcol

    def scores(qi):
        d0 = qi * ta
        out = []
        for mp in range(2):
            qz = q_scr[mp, d0:d0 + ta, :]
            diag = lax.dot_general(k_scr[mp, d0:d0 + ta, :], qz, nt_dims,
                                   preferred_element_type=F32)
            diag = jnp.where(causal, diag, -1e30)
            past = None
            if qi > 0:
                past = lax.dot_general(k_scr[mp, 0:d0, :], qz, nt_dims,
                                       preferred_element_type=F32)
            out.append((past, diag))
        return out

    def finish(qi, sc, exact_max):
        d0 = qi * ta
        outs = []
        for past, diag in sc:
            if exact_max:
                m = jnp.max(diag, axis=0, keepdims=True)
                if past is not None:
                    m = jnp.maximum(m, jnp.max(past, axis=0, keepdims=True))
                diag = diag - m
                if past is not None:
                    past = past - m
            acc = jnp.dot(vt_scr[:, d0:d0 + ta], jnp.exp2(diag).astype(BF16),
                          preferred_element_type=F32)
            if past is not None:
                acc = acc + jnp.dot(vt_scr[:, 0:d0], jnp.exp2(past).astype(BF16),
                                    preferred_element_type=F32)
            outs.append(acc[0:LANES] / acc[LANES:LANES + 1])
        o = (outs[0] - lam * outs[1]).T
        ms = jnp.mean(o * o, axis=-1, keepdims=True)
        o_ref[0, d0:d0 + ta, :] = (o * lax.rsqrt(ms + EPS) * sw_ref[...]
                                   * (1.0 - lambda_init)).astype(BF16)

    def run(exact_max):
        prepare()
        sc = scores(0)
        for qi in range(nt):
            nxt = scores(qi + 1) if qi + 1 < nt else None
            finish(qi, sc, exact_max)
            sc = nxt

    @pl.when(bound_ok)
    def _():
        run(False)

    @pl.when(jnp.logical_not(bound_ok))
    def _():
        run(True)


def _attn(u3, rope_tabs, qnw, knw, lam4, subln_w, lambda_init, ta):
    b, seq, _ = u3.shape
    c_t, sa_t, sb_t = rope_tabs
    nt = seq // ta
    body = functools.partial(_attn_body, seq=seq, ta=ta, lambda_init=lambda_init)
    qcol, kcol, vcol = 4 * GDN_HEADS, 4 * GDN_HEADS + DIFF_HEADS, 4 * GDN_HEADS + 2 * DIFF_HEADS
    tab = pl.BlockSpec((seq, LANES), lambda i, h: (0, 0))
    vec = pl.BlockSpec((1, LANES), lambda i, h: (0, 0))

    def tok(col0):
        return pl.BlockSpec((1, seq, LANES), lambda i, h: (i, 0, col0 + h))

    return pl.pallas_call(
        body,
        grid=(b, DIFF_HEADS),
        in_specs=[tok(qcol), tok(kcol), tok(vcol), tab, tab, tab, vec, vec,
                  pl.BlockSpec((4, LANES), lambda i, h: (0, 0)), vec],
        out_specs=pl.BlockSpec((1, seq, LANES), lambda i, h: (i, 0, h)),
        out_shape=jax.ShapeDtypeStruct((b, seq, DIFF_HEADS * 2 * DIFF_D), BF16),
        scratch_shapes=[
            pltpu.VMEM((2, seq, LANES), BF16),
            pltpu.VMEM((2, seq, LANES), BF16),
            pltpu.VMEM((LANES + ONES_ROWS, seq), BF16),
        ],
        compiler_params=pltpu.CompilerParams(
            dimension_semantics=("arbitrary", "arbitrary"), vmem_limit_bytes=VMEM_LIMIT),
        name="attn",
    )(u3, u3, u3, c_t, sa_t, sb_t, qnw, knw, lam4, subln_w)


def _mixer_body(x_ref, og_ref, od_ref, ga_ref, gb_ref, bg_ref, wg_ref, wd_ref, wo_ref, o_ref):
    d = x_ref.shape[1]
    ya = jnp.dot(og_ref[...], wg_ref[...], preferred_element_type=F32)
    yb = jnp.dot(od_ref[...], wd_ref[...], preferred_element_type=F32)
    ga = _sigmoid(ga_ref[...].astype(F32) + bg_ref[:, :d])
    gb = _sigmoid(gb_ref[...].astype(F32) + bg_ref[:, d:])
    mix = (ga * ya + gb * yb).astype(BF16)
    o_ref[...] = x_ref[...] + jnp.dot(mix, wo_ref[...], preferred_element_type=F32)


def _mixer(x2d, og, od, u, b_gate, wg, wd, wo, gate_col, tm):
    m, d = x2d.shape
    row = lambda i: (i, 0)
    full = lambda i: (0, 0)
    return pl.pallas_call(
        _mixer_body,
        grid=(m // tm,),
        in_specs=[
            pl.BlockSpec((tm, d), row),
            pl.BlockSpec((tm, d), row),
            pl.BlockSpec((tm, d), row),
            pl.BlockSpec((tm, d), lambda i: (i, gate_col)),
            pl.BlockSpec((tm, d), lambda i: (i, gate_col + 1)),
            pl.BlockSpec((1, 2 * d), full),
            pl.BlockSpec((d, d), full),
            pl.BlockSpec((d, d), full),
            pl.BlockSpec((d, d), full),
        ],
        out_specs=pl.BlockSpec((tm, d), row),
        out_shape=jax.ShapeDtypeStruct((m, d), F32),
        compiler_params=pltpu.CompilerParams(
            dimension_semantics=("arbitrary",), vmem_limit_bytes=VMEM_LIMIT),
        name="mixer",
    )(x2d, og, od, u, u, b_gate, wg, wd, wo)


def _ffn_body(x_ref, n2_ref, wup_ref, cw_ref, cb_ref, wdn_ref, o_ref, carry_scr, *, tpb, tf):
    i = pl.program_id(0)
    dff = wdn_ref.shape[0]
    x = x_ref[...]
    ms = jnp.mean(x * x, axis=-1, keepdims=True)
    hb = (x * lax.rsqrt(ms + EPS) * n2_ref[...]).astype(BF16)
    tm = x.shape[0]
    first = (i % tpb) == 0

    def conv(u, cols):
        prev = jnp.where(first, 0.0, carry_scr[:, cols])
        w = cw_ref[:, cols]
        acc = u * w[FFN_CONV - 1:FFN_CONV] + cb_ref[:, cols]
        for s in range(1, FFN_CONV):
            acc = acc + _shift_rows(u, prev, s) * w[FFN_CONV - 1 - s:FFN_CONV - s]
        carry_scr[:, cols] = u[tm - 8:tm]
        return acc

    acc = x
    for f in range(dff // tf):
        gcols = slice(f * tf, (f + 1) * tf)
        vcols = slice(dff + f * tf, dff + (f + 1) * tf)
        ug = jnp.dot(hb, wup_ref[:, gcols], preferred_element_type=F32)
        uv = jnp.dot(hb, wup_ref[:, vcols], preferred_element_type=F32)
        act = (_silu(conv(ug, gcols)) * conv(uv, vcols)).astype(BF16)
        acc = acc + jnp.dot(act, wdn_ref[gcols, :], preferred_element_type=F32)
    o_ref[...] = acc


def _ffn(x2d, n2, wup, cw, cb, wdn, seq, tm, tf):
    m, d = x2d.shape
    dff = wdn.shape[0]
    full = lambda i: (0, 0)
    body = functools.partial(_ffn_body, tpb=seq // tm, tf=tf)
    return pl.pallas_call(
        body,
        grid=(m // tm,),
        in_specs=[
            pl.BlockSpec((tm, d), lambda i: (i, 0)),
            pl.BlockSpec((1, d), full),
            pl.BlockSpec((d, 2 * dff), full, pipeline_mode=pl.Buffered(1)),
            pl.BlockSpec((FFN_CONV, 2 * dff), full),
            pl.BlockSpec((1, 2 * dff), full),
            pl.BlockSpec((dff, d), full, pipeline_mode=pl.Buffered(1)),
        ],
        out_specs=pl.BlockSpec((tm, d), lambda i: (i, 0)),
        out_shape=jax.ShapeDtypeStruct((m, d), F32),
        scratch_shapes=[pltpu.VMEM((8, 2 * dff), F32)],
        compiler_params=pltpu.CompilerParams(
            dimension_semantics=("arbitrary",), vmem_limit_bytes=VMEM_LIMIT),
        name="ffn",
    )(x2d, n2, wup, cw, cb, wdn)


def _rope_tables(seq):
    half = ROPE_DIM // 2
    pos = jnp.arange(seq, dtype=F32)
    inv_freq = ROPE_THETA ** (-jnp.arange(0, ROPE_DIM, 2, dtype=F32) / ROPE_DIM)
    ang = pos[:, None] * inv_freq[None, :]
    cos, sin = jnp.cos(ang), jnp.sin(ang)
    ones = jnp.ones((seq, DIFF_D - ROPE_DIM), F32)
    zeros = jnp.zeros((seq, DIFF_D - ROPE_DIM), F32)
    zh = jnp.zeros((seq, half), F32)
    c = jnp.concatenate([cos, cos, ones], axis=-1)
    sa = jnp.concatenate([-sin, zh, zeros], axis=-1)
    sb = jnp.concatenate([zh, sin, zeros], axis=-1)
    return tuple(jnp.concatenate([t, t], axis=-1) for t in (c, sa, sb))


def _pad_lanes(v, offset):
    return jnp.zeros((LANES,), F32).at[offset:offset + v.shape[0]].set(v.astype(F32))


def kernel(x, norm1_w, w_in, b_gate, gdn_conv_w, gdn_A_log, gdn_dt_bias, gdn_norm_w,
           diff_q_norm_w, diff_k_norm_w, lambda_q1, lambda_k1, lambda_q2, lambda_k2,
           diff_subln_w, w_gdn_out, w_diff_out, w_o, norm2_w, w_up, ffn_conv_w,
           ffn_conv_b, w_down):
    b, seq, d = x.shape
    depth = norm1_w.shape[0]
    gqk = GDN_HEADS * GDN_D
    dqk = DIFF_HEADS * 2 * DIFF_D
    o_b = 4 * gqk
    o_dq = o_b + 2 * GDN_HEADS
    o_gate = o_dq + 3 * dqk
    dff = w_down.shape[1]
    rope_tabs = _rope_tables(seq)
    x2d = x.reshape(b * seq, d)
    tm_in = min(2048, seq)
    tq = min(256, seq)
    tm_mix = min(512, seq)
    tm_ffn = min(512, seq)
    tf = dff

    for layer in range(depth):
        lambda_init = 0.8 - 0.6 * math.exp(-0.3 * layer)
        wl = w_in[layer]
        w_main = jnp.concatenate([wl[:, :o_b], wl[:, o_dq:]], axis=1).astype(BF16)
        w_ba = jnp.pad(wl[:, o_b:o_dq], ((0, 0), (0, LANES - 2 * GDN_HEADS))).astype(BF16)
        gp = jnp.stack([_pad_lanes(gdn_A_log[layer], GDN_HEADS),
                        _pad_lanes(gdn_dt_bias[layer], GDN_HEADS)])
        u, bgc, gt = _inproj(x2d, norm1_w[layer][None, :], w_main, w_ba, gp, seq, tm_in, 1024)
        u3 = u.reshape(b, seq, u.shape[1])
        og = _gdn(u3, gdn_conv_w[layer], bgc.reshape(b, seq, LANES), gt,
                  gdn_norm_w[layer][None, :], hb=4)
        qnw = jnp.tile(diff_q_norm_w[layer], 2)[None, :]
        knw = jnp.tile(diff_k_norm_w[layer], 2)[None, :]
        lam4 = jnp.stack([jnp.pad(v[layer], (0, LANES - DIFF_D))
                          for v in (lambda_q1, lambda_k1, lambda_q2, lambda_k2)])
        od = _attn(u3, rope_tabs, qnw, knw, lam4, diff_subln_w[layer][None, :], lambda_init, tq)
        x2d = _mixer(x2d, og.reshape(b * seq, gqk), od.reshape(b * seq, dqk), u,
                     b_gate[layer][None, :], w_gdn_out[layer].astype(BF16),
                     w_diff_out[layer].astype(BF16), w_o[layer].astype(BF16),
                     (4 * gqk + 3 * dqk) // d, tm_mix)
        x2d = _ffn(x2d, norm2_w[layer][None, :], w_up[layer].astype(BF16), ffn_conv_w[layer],
                   ffn_conv_b[layer][None, :], w_down[layer].astype(BF16), seq, tm_ffn, tf)
    return x2d.reshape(b, seq, d)
```

```python
import functools
import math

import jax
import jax.numpy as jnp
from jax import lax
from jax.experimental import pallas as pl
from jax.experimental.pallas import tpu as pltpu

F32 = jnp.float32
BF16 = jnp.bfloat16

EPS = 1e-6
LANES = 128
GDN_HEADS = 8
GDN_D = 128
GDN_CONV = 4
CHUNK = 64
GROUP = 256
GROUPS_PER_STEP = 1
DIFF_HEADS = 8
DIFF_D = 64
ROPE_DIM = DIFF_D // 4
ROPE_THETA = 500000.0
FFN_CONV = 3
ONES_ROWS = 16
BOUND_SLACK = 1.01
MAX_SAFE_SHIFT = 50.0
LOG2E = math.log2(math.e)
VMEM_LIMIT = 56 * 1024 * 1024


def _mm(a, b):
    return jnp.dot(a.astype(BF16), b.astype(BF16), preferred_element_type=F32)


def _mm_nt(a, b):
    return lax.dot_general(a.astype(BF16), b.astype(BF16), (((1,), (1,)), ((), ())),
                           preferred_element_type=F32)


def _mm_tn(a, b):
    return lax.dot_general(a.astype(BF16), b.astype(BF16), (((0,), (0,)), ((), ())),
                           preferred_element_type=F32)


def _sigmoid(x):
    return 0.5 * jnp.tanh(0.5 * x) + 0.5


def _silu(x):
    h = 0.5 * x
    return h + h * jnp.tanh(h)


def _shift_rows(cur, prev8, s):
    tiles = cur.reshape(cur.shape[0] // 8, 8, cur.shape[1])
    rot = pltpu.roll(tiles, s, axis=1)
    before = jnp.concatenate([pltpu.roll(prev8, s, axis=0)[None], rot[:-1]], axis=0)
    row = lax.broadcasted_iota(jnp.int32, tiles.shape, 1)
    return jnp.where(row < s, before, rot).reshape(cur.shape)


def _inproj_body(x_ref, n1_ref, w_ref, wba_ref, gp_ref, cw_ref, u_ref, bgc_ref, gt_ref, h_scr,
                 *, n_conv):
    j = pl.program_id(1)

    @pl.when(j == 0)
    def _():
        x = x_ref[...]
        ms = jnp.mean(x * x, axis=-1, keepdims=True)
        hb = (x * lax.rsqrt(ms + EPS) * n1_ref[...]).astype(BF16)
        h_scr[...] = hb
        ba = jnp.dot(hb, wba_ref[...], preferred_element_type=F32)
        tm = ba.shape[0]
        lane = lax.broadcasted_iota(jnp.int32, ba.shape, 1)
        row = lax.broadcasted_iota(jnp.int32, ba.shape, 0)
        beta = _sigmoid(ba)
        zz = ba + gp_ref[1:2, :]
        softplus = jnp.maximum(zz, 0.0) + jnp.log1p(jnp.exp(-jnp.abs(zz)))
        g = -(jnp.exp(gp_ref[0:1, :]) * LOG2E) * softplus
        rin = row & (CHUNK - 1)
        gc = g
        s = 1
        while s < CHUNK:
            gc = gc + jnp.where(rin >= s, pltpu.roll(gc, s, axis=0), 0.0)
            s *= 2
        glast = jnp.broadcast_to(
            gc.reshape(tm // CHUNK, CHUNK, LANES)[:, CHUNK - 1:CHUNK, :],
            (tm // CHUNK, CHUNK, LANES)).reshape(tm, LANES)
        glast = pltpu.roll(glast, GDN_HEADS, axis=1)
        out = jnp.where(lane < GDN_HEADS, beta,
                        jnp.where(lane < 2 * GDN_HEADS, gc,
                                  jnp.where(lane < 3 * GDN_HEADS, glast, 0.0)))
        bgc_ref[...] = out
        gt_ref[0] = out.T[GDN_HEADS:2 * GDN_HEADS, :]

    tm = h_scr.shape[0]
    nch = tm // GROUP

    def chunk_dot(c):
        return jnp.dot(h_scr[c * GROUP:(c + 1) * GROUP, :], w_ref[...], preferred_element_type=F32)

    def sweep(epilogue):
        nxt = chunk_dot(0)
        prev = jnp.zeros((8, w_ref.shape[1]), F32)
        for c in range(nch):
            cur = nxt
            if c + 1 < nch:
                nxt = chunk_dot(c + 1)
            u_ref[c * GROUP:(c + 1) * GROUP, :] = epilogue(cur, prev).astype(BF16)
            prev = cur[GROUP - 8:GROUP]

    def conv_silu(cur, prev):
        w = cw_ref[...]
        acc = cur * w[GDN_CONV - 1:GDN_CONV]
        for s in range(1, GDN_CONV):
            acc = acc + _shift_rows(cur, prev, s) * w[GDN_CONV - 1 - s:GDN_CONV - s]
        return _silu(acc)

    @pl.when(j < n_conv)
    def _():
        sweep(conv_silu)

    @pl.when(j >= n_conv)
    def _():
        sweep(lambda cur, prev: cur)


def _inproj(x2d, n1, w, wba, gp, conv_w, seq, tm, tn):
    m, d = x2d.shape
    n = w.shape[1]
    assert tm == seq and conv_w.shape[1] % tn == 0
    tpb = seq // tm
    n_conv = conv_w.shape[1] // tn
    return pl.pallas_call(
        functools.partial(_inproj_body, n_conv=n_conv),
        grid=(m // tm, n // tn),
        in_specs=[
            pl.BlockSpec((tm, d), lambda i, j: (i, 0)),
            pl.BlockSpec((1, d), lambda i, j: (0, 0)),
            pl.BlockSpec((d, tn), lambda i, j: (0, j)),
            pl.BlockSpec((d, LANES), lambda i, j: (0, 0)),
            pl.BlockSpec((2, LANES), lambda i, j: (0, 0)),
            pl.BlockSpec((GDN_CONV, tn), lambda i, j: (0, jnp.minimum(j, n_conv - 1))),
        ],
        out_specs=[
            pl.BlockSpec((tm, tn), lambda i, j: (i, j)),
            pl.BlockSpec((tm, LANES), lambda i, j: (i, 0)),
            pl.BlockSpec((1, GDN_HEADS, tm), lambda i, j: (i // tpb, 0, i % tpb)),
        ],
        out_shape=[
            jax.ShapeDtypeStruct((m, n), BF16),
            jax.ShapeDtypeStruct((m, LANES), F32),
            jax.ShapeDtypeStruct((m // seq, GDN_HEADS, seq), F32),
        ],
        scratch_shapes=[pltpu.VMEM((tm, d), BF16)],
        compiler_params=pltpu.CompilerParams(
            dimension_semantics=("arbitrary", "arbitrary"), vmem_limit_bytes=VMEM_LIMIT),
        name="inproj",
    )(x2d, n1, w, wba, gp, conv_w)


def _gdn_body(q_ref, k_ref, v_ref, z_ref, bgc_ref, gt_ref, nw_ref,
              o_ref, mb_scr, oq_scr, dl_scr, *, hb, seq):
    hbase = pl.program_id(1) * hb
    ngroups = seq // GROUP
    nchunks = seq // CHUNK
    cpg = GROUP // CHUNK

    ri = lax.broadcasted_iota(jnp.int32, (GROUP, GROUP), 0)
    ci = lax.broadcasted_iota(jnp.int32, (GROUP, GROUP), 1)
    tri = ci <= ri
    strict = ci < ri
    same64 = (ri >> 6) == (ci >> 6)

    def block_mask(shift):
        return (ri >> shift) == (ci >> shift)

    level_masks = [(strict & block_mask(3)).astype(BF16)] + [
        (strict & block_mask(s + 1) & jnp.logical_not(block_mask(s))).astype(BF16) for s in (3, 4, 5)]
    eye = (ri == ci).astype(F32)
    lane = lax.broadcasted_iota(jnp.int32, (GROUP, LANES), 1)

    def l2n(x):
        return x * lax.rsqrt(jnp.sum(x * x, axis=-1, keepdims=True) + EPS)

    def group_step(sg):
        units = [(hh, sg * GROUPS_PER_STEP + sub) for sub in range(GROUPS_PER_STEP)
                 for hh in range(hb)]

        def prep(hh, g):
            r0 = pl.multiple_of(g * GROUP, GROUP)
            rows = pl.ds(r0, GROUP)
            blk = bgc_ref[0, rows, :]
            lanes = slice(hh * LANES, (hh + 1) * LANES)
            head = hbase + hh
            qn = l2n(q_ref[0, rows, lanes].astype(F32)) * (GDN_D ** -0.5)
            kn = l2n(k_ref[0, rows, lanes].astype(F32))
            vn = v_ref[0, rows, lanes].astype(F32)

            def col(off):
                return jnp.sum(jnp.where(lane == head + off, blk, 0.0), axis=-1, keepdims=True)

            beta = col(0)
            gc = col(GDN_HEADS)
            glast = col(2 * GDN_HEADS)
            gc_row = gt_ref[0, pl.ds(head, 1), rows]
            eg = jnp.exp2(gc)
            kb = kn * beta
            rhs = jnp.concatenate([(vn * beta).astype(BF16), (kb * eg).astype(BF16)], axis=1)
            q_dec = qn * eg
            k_dec = (kn * jnp.exp2(glast - gc)).astype(BF16)
            kn_b = kn.astype(BF16)
            kk = _mm_nt(kb, kn_b)
            qk = _mm_nt(qn, kn_b)
            decay = jnp.exp2(jnp.where(same64 & tri, gc - gc_row, -1e30))
            lmat = (kk * decay).astype(BF16)
            intra = (qk * decay).astype(BF16)
            return lmat, intra, rhs, q_dec, k_dec, glast

        pre = [prep(hh, g) for hh, g in units]
        us = range(len(units))
        lm = [p[0] for p in pre]

        d8 = [lm[i] * level_masks[0] for i in us]
        d8_2 = [_mm(d8[i], d8[i]).astype(BF16) for i in us]
        d8_4 = [_mm(d8_2[i], d8_2[i]) for i in us]
        x = [eye - d8[i].astype(F32) for i in us]
        x = [x[i] + _mm(x[i], d8_2[i]) for i in us]
        x = [x[i] + _mm(x[i], d8_4[i]) for i in us]
        for level in level_masks[1:]:
            xb = [x[i].astype(BF16) for i in us]
            t = [_mm(lm[i] * level, xb[i]) for i in us]
            x = [x[i] - _mm(xb[i], t[i]) for i in us]

        uw = [_mm(x[i], pre[i][2]).astype(BF16) for i in us]
        iu = [_mm(pre[i][1], uw[i]) for i in us]
        for i, (h, g) in enumerate(units):
            _, _, _, q_dec, k_dec, glast = pre[i]
            rows = pl.ds(pl.multiple_of(g * GROUP, GROUP), GROUP)
            oq_scr[h, rows, :] = jnp.concatenate(
                [iu[i][:, :GDN_D], q_dec - iu[i][:, GDN_D:]], axis=1).astype(BF16)
            for c in range(cpg):
                cs = slice(c * CHUNK, (c + 1) * CHUNK)
                bm = _mm_tn(k_dec[cs], uw[i][cs])
                n = g * cpg + c
                mb_scr[h, n] = bm.astype(BF16)
                dl_scr[h, n] = jnp.broadcast_to(
                    jnp.exp2(glast[c * CHUNK:c * CHUNK + 8]), (8, LANES))

    nw = nw_ref[...]

    def chunk_step(n, states):
        rows = pl.ds(pl.multiple_of(n * CHUNK, CHUNK), CHUNK)
        new_states = []
        for hh in range(hb):
            lanes = slice(hh * LANES, (hh + 1) * LANES)
            s = states[hh]
            sb = s.astype(BF16)
            oq = oq_scr[hh, rows, :]
            mb = mb_scr[hh, n]
            o = (jnp.dot(oq[:, GDN_D:], sb, preferred_element_type=F32)
                 + oq[:, :GDN_D].astype(F32))
            ms = jnp.mean(o * o, axis=-1, keepdims=True)
            zz = z_ref[0, rows, lanes].astype(F32)
            o_ref[0, rows, lanes] = (o * lax.rsqrt(ms + EPS) * nw * _silu(zz)).astype(BF16)
            s = (s * dl_scr[hh, n][0:1, :]
                 - jnp.dot(mb[:, GDN_D:], sb, preferred_element_type=F32)
                 + mb[:, :GDN_D].astype(F32))
            new_states.append(s)
        return tuple(new_states)

    def recur(sg, states):
        for c in range(GROUPS_PER_STEP * cpg):
            states = chunk_step(sg * (GROUPS_PER_STEP * cpg) + c, states)
        return states

    def fused_step(sg, states):
        states = recur(sg - 1, states)
        group_step(sg)
        return states

    nsteps = ngroups // GROUPS_PER_STEP
    group_step(0)
    init = tuple(jnp.zeros((GDN_D, GDN_D), F32) for _ in range(hb))
    states = lax.fori_loop(1, nsteps, fused_step, init)
    recur(nsteps - 1, states)


def _gdn(u3, bgc3, gt, norm_w, hb):
    b, seq, _ = u3.shape
    w = hb * LANES
    nblk = GDN_HEADS // hb
    nchunks = seq // CHUNK
    body = functools.partial(_gdn_body, hb=hb, seq=seq)

    def tok(col0):
        return pl.BlockSpec((1, seq, w), lambda i, j: (i, 0, col0 + j))

    return pl.pallas_call(
        body,
        grid=(b, nblk),
        in_specs=[
            tok(0), tok(nblk), tok(2 * nblk), tok(3 * nblk),
            pl.BlockSpec((1, seq, LANES), lambda i, j: (i, 0, 0)),
            pl.BlockSpec((1, GDN_HEADS, seq), lambda i, j: (i, 0, 0)),
            pl.BlockSpec((1, GDN_D), lambda i, j: (0, 0)),
        ],
        out_specs=pl.BlockSpec((1, seq, w), lambda i, j: (i, 0, j)),
        out_shape=jax.ShapeDtypeStruct((b, seq, GDN_HEADS * GDN_D), BF16),
        scratch_shapes=[
            pltpu.VMEM((hb, nchunks, GDN_D, 2 * GDN_D), BF16),
            pltpu.VMEM((hb, seq, 2 * GDN_D), BF16),
            pltpu.VMEM((hb, nchunks, 8, LANES), F32),
        ],
        compiler_params=pltpu.CompilerParams(
            dimension_semantics=("arbitrary", "arbitrary"), vmem_limit_bytes=VMEM_LIMIT),
        name="gdn",
    )(u3, u3, u3, u3, bgc3, gt, norm_w)


def _attn_body(q_ref, k_ref, v_ref, c_ref, sa_ref, sb_ref, qnw_ref, knw_ref, lam_ref, sw_ref,
               o_ref, k_scr, q_scr, vt_scr, *, seq, ta, lambda_init):
    nt = seq // ta
    low = lax.broadcasted_iota(jnp.int32, (ta, LANES), 1) < DIFF_D

    lane = lax.broadcasted_iota(jnp.int32, (ta, LANES), 1)
    ri = lax.broadcasted_iota(jnp.int32, (LANES, LANES), 0)
    ci = lax.broadcasted_iota(jnp.int32, (LANES, LANES), 1)
    eye = (ri == ci).astype(BF16)
    same_half = ((ri < DIFF_D) == (ci < DIFF_D)).astype(BF16)
    nt_dims = (((1,), (1,)), ((), ()))

    def norm_rope(x, w, rows):
        ms = jnp.dot((x * x).astype(BF16), same_half, preferred_element_type=F32) * (1.0 / DIFF_D)
        y = x * lax.rsqrt(ms + EPS) * w
        half = ROPE_DIM // 2
        return (y * c_ref[rows, :] + pltpu.roll(y, LANES - half, axis=1) * sa_ref[rows, :]
                + pltpu.roll(y, half, axis=1) * sb_ref[rows, :])

    qscale = (DIFF_D ** -0.5) * math.log2(math.e)
    wq_max = jnp.max(jnp.abs(qnw_ref[...]), axis=-1, keepdims=True)
    wk_max = jnp.max(jnp.abs(knw_ref[...]), axis=-1, keepdims=True)
    shift = (BOUND_SLACK * DIFF_D * qscale) * wq_max * wk_max
    bound_ok = jnp.max(shift) <= MAX_SAFE_SHIFT
    s_hi = shift.astype(BF16).astype(F32)
    s_mid = (shift - s_hi).astype(BF16).astype(F32)
    s_lo = shift - s_hi - s_mid
    shift_lane = (DIFF_D, 0)

    def shift_terms(base):
        return jnp.where(lane == base, -s_hi,
                         jnp.where(lane == base + 1, -s_mid, jnp.where(lane == base + 2, -s_lo, 0.0)))

    def ones_lanes(base):
        return jnp.where((lane >= base) & (lane < base + 3), 1.0, 0.0)

    def prepare():
        for t in range(nt):
            rows = slice(t * ta, (t + 1) * ta)
            kx = norm_rope(k_ref[0, rows, :].astype(F32), knw_ref[...], rows)
            k_scr[0, rows, :] = jnp.where(low, kx, ones_lanes(shift_lane[0])).astype(BF16)
            k_scr[1, rows, :] = jnp.where(low, ones_lanes(shift_lane[1]), kx).astype(BF16)
            qx = norm_rope(q_ref[0, rows, :].astype(F32), qnw_ref[...], rows) * qscale
            q_scr[0, rows, :] = jnp.where(low, qx, shift_terms(shift_lane[0])).astype(BF16)
            q_scr[1, rows, :] = jnp.where(low, shift_terms(shift_lane[1]), qx).astype(BF16)
            vt_scr[0:LANES, rows] = lax.dot_general(eye, v_ref[0, rows, :], nt_dims,
                                                    preferred_element_type=F32).astype(BF16)
        vt_scr[LANES:, :] = jnp.ones((ONES_ROWS, seq), BF16)

    lam = (jnp.exp(jnp.sum(lam_ref[0:1, :] * lam_ref[1:2, :], axis=-1, keepdims=True))
           - jnp.exp(jnp.sum(lam_ref[2:3, :] * lam_ref[3:4, :], axis=-1, keepdims=True))
           + lambda_init)
    krow = lax.broadcasted_iota(jnp.int32, (ta, ta), 0)
    qcol = lax.broadcasted_iota(jnp.int32, (ta, ta), 1)
    causal = krow <= qcol

    def scores(qi):
        d0 = qi * ta
        out = []
        for mp in range(2):
            qz = q_scr[mp, d0:d0 + ta, :]
            diag = lax.dot_general(k_scr[mp, d0:d0 + ta, :], qz, nt_dims,
                                   preferred_element_type=F32)
            diag = jnp.where(causal, diag, -1e30)
            past = None
            if qi > 0:
                past = lax.dot_general(k_scr[mp, 0:d0, :], qz, nt_dims,
                                       preferred_element_type=F32)
            out.append((past, diag))
        return out

    def finish(qi, sc, exact_max):
        d0 = qi * ta
        outs = []
        for past, diag in sc:
            if exact_max:
                m = jnp.max(diag, axis=0, keepdims=True)
                if past is not None:
                    m = jnp.maximum(m, jnp.max(past, axis=0, keepdims=True))
                diag = diag - m
                if past is not None:
                    past = past - m
            acc = jnp.dot(vt_scr[:, d0:d0 + ta], jnp.exp2(diag).astype(BF16),
                          preferred_element_type=F32)
            if past is not None:
                acc = acc + jnp.dot(vt_scr[:, 0:d0], jnp.exp2(past).astype(BF16),
                                    preferred_element_type=F32)
            outs.append(acc[0:LANES] / acc[LANES:LANES + 1])
        o = (outs[0] - lam * outs[1]).T
        ms = jnp.mean(o * o, axis=-1, keepdims=True)
        o_ref[0, d0:d0 + ta, :] = (o * lax.rsqrt(ms + EPS) * sw_ref[...]
                                   * (1.0 - lambda_init)).astype(BF16)

    def run(exact_max):
        prepare()
        sc = scores(0)
        for qi in range(nt):
            nxt = scores(qi + 1) if qi + 1 < nt else None
            finish(qi, sc, exact_max)
            sc = nxt

    @pl.when(bound_ok)
    def _():
        run(False)

    @pl.when(jnp.logical_not(bound_ok))
    def _():
        run(True)


def _attn(u3, rope_tabs, qnw, knw, lam4, subln_w, lambda_init, ta):
    b, seq, _ = u3.shape
    c_t, sa_t, sb_t = rope_tabs
    nt = seq // ta
    body = functools.partial(_attn_body, seq=seq, ta=ta, lambda_init=lambda_init)
    qcol, kcol, vcol = 4 * GDN_HEADS, 4 * GDN_HEADS + DIFF_HEADS, 4 * GDN_HEADS + 2 * DIFF_HEADS
    tab = pl.BlockSpec((seq, LANES), lambda i, h: (0, 0))
    vec = pl.BlockSpec((1, LANES), lambda i, h: (0, 0))

    def tok(col0):
        return pl.BlockSpec((1, seq, LANES), lambda i, h: (i, 0, col0 + h))

    return pl.pallas_call(
        body,
        grid=(b, DIFF_HEADS),
        in_specs=[tok(qcol), tok(kcol), tok(vcol), tab, tab, tab, vec, vec,
                  pl.BlockSpec((4, LANES), lambda i, h: (0, 0)), vec],
        out_specs=pl.BlockSpec((1, seq, LANES), lambda i, h: (i, 0, h)),
        out_shape=jax.ShapeDtypeStruct((b, seq, DIFF_HEADS * 2 * DIFF_D), BF16),
        scratch_shapes=[
            pltpu.VMEM((2, seq, LANES), BF16),
            pltpu.VMEM((2, seq, LANES), BF16),
            pltpu.VMEM((LANES + ONES_ROWS, seq), BF16),
        ],
        compiler_params=pltpu.CompilerParams(
            dimension_semantics=("arbitrary", "arbitrary"), vmem_limit_bytes=VMEM_LIMIT),
        name="attn",
    )(u3, u3, u3, c_t, sa_t, sb_t, qnw, knw, lam4, subln_w)


def _mixer_body(x_ref, og_ref, od_ref, ga_ref, gb_ref, bg_ref, wg_ref, wd_ref, wo_ref, o_ref):
    d = x_ref.shape[1]
    ya = jnp.dot(og_ref[...], wg_ref[...], preferred_element_type=F32)
    yb = jnp.dot(od_ref[...], wd_ref[...], preferred_element_type=F32)
    ga = _sigmoid(ga_ref[...].astype(F32) + bg_ref[:, :d])
    gb = _sigmoid(gb_ref[...].astype(F32) + bg_ref[:, d:])
    mix = (ga * ya + gb * yb).astype(BF16)
    o_ref[...] = x_ref[...] + jnp.dot(mix, wo_ref[...], preferred_element_type=F32)


def _mixer(x2d, og, od, u, b_gate, wg, wd, wo, gate_col, tm):
    m, d = x2d.shape
    row = lambda i: (i, 0)
    full = lambda i: (0, 0)
    return pl.pallas_call(
        _mixer_body,
        grid=(m // tm,),
        in_specs=[
            pl.BlockSpec((tm, d), row),
            pl.BlockSpec((tm, d), row),
            pl.BlockSpec((tm, d), row),
            pl.BlockSpec((tm, d), lambda i: (i, gate_col)),
            pl.BlockSpec((tm, d), lambda i: (i, gate_col + 1)),
            pl.BlockSpec((1, 2 * d), full),
            pl.BlockSpec((d, d), full),
            pl.BlockSpec((d, d), full),
            pl.BlockSpec((d, d), full),
        ],
        out_specs=pl.BlockSpec((tm, d), row),
        out_shape=jax.ShapeDtypeStruct((m, d), F32),
        compiler_params=pltpu.CompilerParams(
            dimension_semantics=("arbitrary",), vmem_limit_bytes=VMEM_LIMIT),
        name="mixer",
    )(x2d, og, od, u, u, b_gate, wg, wd, wo)


def _ffn_body(x_ref, n2_ref, wup_ref, cw_ref, cb_ref, wdn_ref, o_ref, carry_scr, *, tpb, tf):
    i = pl.program_id(0)
    dff = wdn_ref.shape[0]
    x = x_ref[...]
    ms = jnp.mean(x * x, axis=-1, keepdims=True)
    hb = (x * lax.rsqrt(ms + EPS) * n2_ref[...]).astype(BF16)
    tm = x.shape[0]
    first = (i % tpb) == 0

    def conv(u, cols):
        prev = jnp.where(first, 0.0, carry_scr[:, cols])
        w = cw_ref[:, cols]
        acc = u * w[FFN_CONV - 1:FFN_CONV] + cb_ref[:, cols]
        for s in range(1, FFN_CONV):
            acc = acc + _shift_rows(u, prev, s) * w[FFN_CONV - 1 - s:FFN_CONV - s]
        carry_scr[:, cols] = u[tm - 8:tm]
        return acc

    acc = x
    for f in range(dff // tf):
        gcols = slice(f * tf, (f + 1) * tf)
        vcols = slice(dff + f * tf, dff + (f + 1) * tf)
        ug = jnp.dot(hb, wup_ref[:, gcols], preferred_element_type=F32)
        uv = jnp.dot(hb, wup_ref[:, vcols], preferred_element_type=F32)
        act = (_silu(conv(ug, gcols)) * conv(uv, vcols)).astype(BF16)
        acc = acc + jnp.dot(act, wdn_ref[gcols, :], preferred_element_type=F32)
    o_ref[...] = acc


def _ffn(x2d, n2, wup, cw, cb, wdn, seq, tm, tf):
    m, d = x2d.shape
    dff = wdn.shape[0]
    full = lambda i: (0, 0)
    body = functools.partial(_ffn_body, tpb=seq // tm, tf=tf)
    return pl.pallas_call(
        body,
        grid=(m // tm,),
        in_specs=[
            pl.BlockSpec((tm, d), lambda i: (i, 0)),
            pl.BlockSpec((1, d), full),
            pl.BlockSpec((d, 2 * dff), full, pipeline_mode=pl.Buffered(1)),
            pl.BlockSpec((FFN_CONV, 2 * dff), full),
            pl.BlockSpec((1, 2 * dff), full),
            pl.BlockSpec((dff, d), full, pipeline_mode=pl.Buffered(1)),
        ],
        out_specs=pl.BlockSpec((tm, d), lambda i: (i, 0)),
        out_shape=jax.ShapeDtypeStruct((m, d), F32),
        scratch_shapes=[pltpu.VMEM((8, 2 * dff), F32)],
        compiler_params=pltpu.CompilerParams(
            dimension_semantics=("arbitrary",), vmem_limit_bytes=VMEM_LIMIT),
        name="ffn",
    )(x2d, n2, wup, cw, cb, wdn)


def _rope_tables(seq):
    half = ROPE_DIM // 2
    pos = jnp.arange(seq, dtype=F32)
    inv_freq = ROPE_THETA ** (-jnp.arange(0, ROPE_DIM, 2, dtype=F32) / ROPE_DIM)
    ang = pos[:, None] * inv_freq[None, :]
    cos, sin = jnp.cos(ang), jnp.sin(ang)
    ones = jnp.ones((seq, DIFF_D - ROPE_DIM), F32)
    zeros = jnp.zeros((seq, DIFF_D - ROPE_DIM), F32)
    zh = jnp.zeros((seq, half), F32)
    c = jnp.concatenate([cos, cos, ones], axis=-1)
    sa = jnp.concatenate([-sin, zh, zeros], axis=-1)
    sb = jnp.concatenate([zh, sin, zeros], axis=-1)
    return tuple(jnp.concatenate([t, t], axis=-1) for t in (c, sa, sb))


def _pad_lanes(v, offset):
    return jnp.zeros((LANES,), F32).at[offset:offset + v.shape[0]].set(v.astype(F32))


def kernel(x, norm1_w, w_in, b_gate, gdn_conv_w, gdn_A_log, gdn_dt_bias, gdn_norm_w,
           diff_q_norm_w, diff_k_norm_w, lambda_q1, lambda_k1, lambda_q2, lambda_k2,
           diff_subln_w, w_gdn_out, w_diff_out, w_o, norm2_w, w_up, ffn_conv_w,
           ffn_conv_b, w_down):
    b, seq, d = x.shape
    depth = norm1_w.shape[0]
    gqk = GDN_HEADS * GDN_D
    dqk = DIFF_HEADS * 2 * DIFF_D
    o_b = 4 * gqk
    o_dq = o_b + 2 * GDN_HEADS
    o_gate = o_dq + 3 * dqk
    dff = w_down.shape[1]
    rope_tabs = _rope_tables(seq)
    x2d = x.reshape(b * seq, d)
    tm_in = min(2048, seq)
    tq = min(256, seq)
    tm_mix = min(512, seq)
    tm_ffn = min(512, seq)
    tf = dff

    for layer in range(depth):
        lambda_init = 0.8 - 0.6 * math.exp(-0.3 * layer)
        wl = w_in[layer]
        w_main = jnp.concatenate([wl[:, :o_b], wl[:, o_dq:]], axis=1).astype(BF16)
        w_ba = jnp.pad(wl[:, o_b:o_dq], ((0, 0), (0, LANES - 2 * GDN_HEADS))).astype(BF16)
        gp = jnp.stack([_pad_lanes(gdn_A_log[layer], GDN_HEADS),
                        _pad_lanes(gdn_dt_bias[layer], GDN_HEADS)])
        u, bgc, gt = _inproj(x2d, norm1_w[layer][None, :], w_main, w_ba, gp, gdn_conv_w[layer],
                              seq, tm_in, 1024)
        u3 = u.reshape(b, seq, u.shape[1])
        og = _gdn(u3, bgc.reshape(b, seq, LANES), gt,
                  gdn_norm_w[layer][None, :], hb=4)
        qnw = jnp.tile(diff_q_norm_w[layer], 2)[None, :]
        knw = jnp.tile(diff_k_norm_w[layer], 2)[None, :]
        lam4 = jnp.stack([jnp.pad(v[layer], (0, LANES - DIFF_D))
                          for v in (lambda_q1, lambda_k1, lambda_q2, lambda_k2)])
        od = _attn(u3, rope_tabs, qnw, knw, lam4, diff_subln_w[layer][None, :], lambda_init, tq)
        x2d = _mixer(x2d, og.reshape(b * seq, gqk), od.reshape(b * seq, dqk), u,
                     b_gate[layer][None, :], w_gdn_out[layer].astype(BF16),
                     w_diff_out[layer].astype(BF16), w_o[layer].astype(BF16),
                     (4 * gqk + 3 * dqk) // d, tm_mix)
        x2d = _ffn(x2d, norm2_w[layer][None, :], w_up[layer].astype(BF16), ffn_conv_w[layer],
                   ffn_conv_b[layer][None, :], w_down[layer].astype(BF16), seq, tm_ffn, tf)
    return x2d.reshape(b, seq, d)
```

```python
import functools
import math

import jax
import jax.numpy as jnp
from jax import lax
from jax.experimental import pallas as pl
from jax.experimental.pallas import tpu as pltpu

F32 = jnp.float32
BF16 = jnp.bfloat16

EPS = 1e-6
LANES = 128
GDN_HEADS = 8
GDN_D = 128
GDN_CONV = 4
CHUNK = 64
GROUP = 256
GROUPS_PER_STEP = 2
DIFF_HEADS = 8
DIFF_D = 64
ROPE_DIM = DIFF_D // 4
ROPE_THETA = 500000.0
FFN_CONV = 3
ONES_ROWS = 16
BOUND_SLACK = 1.01
MAX_SAFE_SHIFT = 50.0
LOG2E = math.log2(math.e)
VMEM_LIMIT = 56 * 1024 * 1024


def _mm(a, b):
    return jnp.dot(a.astype(BF16), b.astype(BF16), preferred_element_type=F32)


def _mm_nt(a, b):
    return lax.dot_general(a.astype(BF16), b.astype(BF16), (((1,), (1,)), ((), ())),
                           preferred_element_type=F32)


def _mm_tn(a, b):
    return lax.dot_general(a.astype(BF16), b.astype(BF16), (((0,), (0,)), ((), ())),
                           preferred_element_type=F32)


def _sigmoid(x):
    return 0.5 * jnp.tanh(0.5 * x) + 0.5


def _silu(x):
    h = 0.5 * x
    return h + h * jnp.tanh(h)


def _shift_rows(cur, prev8, s):
    tiles = cur.reshape(cur.shape[0] // 8, 8, cur.shape[1])
    rot = pltpu.roll(tiles, s, axis=1)
    before = jnp.concatenate([pltpu.roll(prev8, s, axis=0)[None], rot[:-1]], axis=0)
    row = lax.broadcasted_iota(jnp.int32, tiles.shape, 1)
    return jnp.where(row < s, before, rot).reshape(cur.shape)


def _inproj_body(x_ref, n1_ref, w_ref, wba_ref, gp_ref, cw_ref, u_ref, bgc_ref, gt_ref, h_scr,
                 *, n_conv):
    j = pl.program_id(1)

    @pl.when(j == 0)
    def _():
        x = x_ref[...]
        ms = jnp.mean(x * x, axis=-1, keepdims=True)
        hb = (x * lax.rsqrt(ms + EPS) * n1_ref[...]).astype(BF16)
        h_scr[...] = hb
        ba = jnp.dot(hb, wba_ref[...], preferred_element_type=F32)
        tm = ba.shape[0]
        lane = lax.broadcasted_iota(jnp.int32, ba.shape, 1)
        row = lax.broadcasted_iota(jnp.int32, ba.shape, 0)
        beta = _sigmoid(ba)
        zz = ba + gp_ref[1:2, :]
        softplus = jnp.maximum(zz, 0.0) + jnp.log1p(jnp.exp(-jnp.abs(zz)))
        g = -(jnp.exp(gp_ref[0:1, :]) * LOG2E) * softplus
        rin = row & (CHUNK - 1)
        gc = g
        s = 1
        while s < CHUNK:
            gc = gc + jnp.where(rin >= s, pltpu.roll(gc, s, axis=0), 0.0)
            s *= 2
        glast = jnp.broadcast_to(
            gc.reshape(tm // CHUNK, CHUNK, LANES)[:, CHUNK - 1:CHUNK, :],
            (tm // CHUNK, CHUNK, LANES)).reshape(tm, LANES)
        glast = pltpu.roll(glast, GDN_HEADS, axis=1)
        out = jnp.where(lane < GDN_HEADS, beta,
                        jnp.where(lane < 2 * GDN_HEADS, gc,
                                  jnp.where(lane < 3 * GDN_HEADS, glast, 0.0)))
        bgc_ref[...] = out
        gt_ref[0] = out.T[GDN_HEADS:2 * GDN_HEADS, :]

    tm = h_scr.shape[0]
    nch = tm // GROUP

    def chunk_dot(c):
        return jnp.dot(h_scr[c * GROUP:(c + 1) * GROUP, :], w_ref[...], preferred_element_type=F32)

    def sweep(epilogue):
        nxt = chunk_dot(0)
        prev = jnp.zeros((8, w_ref.shape[1]), F32)
        for c in range(nch):
            cur = nxt
            if c + 1 < nch:
                nxt = chunk_dot(c + 1)
            u_ref[c * GROUP:(c + 1) * GROUP, :] = epilogue(cur, prev).astype(BF16)
            prev = cur[GROUP - 8:GROUP]

    def conv_silu(cur, prev):
        w = 0.5 * cw_ref[...]
        h = cur * w[GDN_CONV - 1:GDN_CONV]
        for s in range(1, GDN_CONV):
            h = h + _shift_rows(cur, prev, s) * w[GDN_CONV - 1 - s:GDN_CONV - s]
        return h + h * jnp.tanh(h)

    @pl.when(j < n_conv)
    def _():
        sweep(conv_silu)

    @pl.when(j >= n_conv)
    def _():
        sweep(lambda cur, prev: cur)


def _inproj(x2d, n1, w, wba, gp, conv_w, seq, tm, tn):
    m, d = x2d.shape
    n = w.shape[1]
    assert tm == seq and conv_w.shape[1] % tn == 0
    tpb = seq // tm
    n_conv = conv_w.shape[1] // tn
    return pl.pallas_call(
        functools.partial(_inproj_body, n_conv=n_conv),
        grid=(m // tm, n // tn),
        in_specs=[
            pl.BlockSpec((tm, d), lambda i, j: (i, 0)),
            pl.BlockSpec((1, d), lambda i, j: (0, 0)),
            pl.BlockSpec((d, tn), lambda i, j: (0, j)),
            pl.BlockSpec((d, LANES), lambda i, j: (0, 0)),
            pl.BlockSpec((2, LANES), lambda i, j: (0, 0)),
            pl.BlockSpec((GDN_CONV, tn), lambda i, j: (0, jnp.minimum(j, n_conv - 1))),
        ],
        out_specs=[
            pl.BlockSpec((tm, tn), lambda i, j: (i, j)),
            pl.BlockSpec((tm, LANES), lambda i, j: (i, 0)),
            pl.BlockSpec((1, GDN_HEADS, tm), lambda i, j: (i // tpb, 0, i % tpb)),
        ],
        out_shape=[
            jax.ShapeDtypeStruct((m, n), BF16),
            jax.ShapeDtypeStruct((m, LANES), F32),
            jax.ShapeDtypeStruct((m // seq, GDN_HEADS, seq), F32),
        ],
        scratch_shapes=[pltpu.VMEM((tm, d), BF16)],
        compiler_params=pltpu.CompilerParams(
            dimension_semantics=("arbitrary", "arbitrary"), vmem_limit_bytes=VMEM_LIMIT),
        name="inproj",
    )(x2d, n1, w, wba, gp, conv_w)


def _gdn_body(q_ref, k_ref, v_ref, z_ref, bgc_ref, gt_ref, nw_ref,
              o_ref, mb_scr, oq_scr, dl_scr, *, hb, seq):
    hbase = pl.program_id(1) * hb
    ngroups = seq // GROUP
    nchunks = seq // CHUNK
    cpg = GROUP // CHUNK

    ri = lax.broadcasted_iota(jnp.int32, (GROUP, GROUP), 0)
    ci = lax.broadcasted_iota(jnp.int32, (GROUP, GROUP), 1)
    tri = ci <= ri
    same64 = (ri >> 6) == (ci >> 6)

    def selectors(r, c):
        def same(shift):
            return (r >> shift) == (c >> shift)
        first = (c < r) & same(3)
        return [first.astype(BF16)] + [
            ((c < r) & same(s + 1) & jnp.logical_not(same(s))).astype(BF16) for s in (3, 4, 5)]

    rs = lax.broadcasted_iota(jnp.int32, (CHUNK, GROUP), 0)
    cs = lax.broadcasted_iota(jnp.int32, (CHUNK, GROUP), 1)
    level_masks = selectors(ri, ci)
    side_masks = selectors(rs, cs & (CHUNK - 1))
    eye_s = (rs == (cs & (CHUNK - 1))).astype(F32)
    chunk_cols = [((cs >> 6) == c).astype(BF16) for c in range(cpg)]
    lane = lax.broadcasted_iota(jnp.int32, (GROUP, LANES), 1)

    def to_side(a):
        out = a[0:CHUNK]
        for c in range(1, cpg):
            out = out + a[c * CHUNK:(c + 1) * CHUNK]
        return out

    def to_blockdiag(a):
        return jnp.concatenate([a * chunk_cols[c] for c in range(cpg)], axis=0)

    def l2n(x):
        return x * lax.rsqrt(jnp.sum(x * x, axis=-1, keepdims=True) + EPS)

    def group_step(sg):
        units = [(hh, sg * GROUPS_PER_STEP + sub) for sub in range(GROUPS_PER_STEP)
                 for hh in range(hb)]

        def prep(hh, g):
            r0 = pl.multiple_of(g * GROUP, GROUP)
            rows = pl.ds(r0, GROUP)
            blk = bgc_ref[0, rows, :]
            lanes = slice(hh * LANES, (hh + 1) * LANES)
            head = hbase + hh
            qn = l2n(q_ref[0, rows, lanes].astype(F32)) * (GDN_D ** -0.5)
            kn = l2n(k_ref[0, rows, lanes].astype(F32))
            vn = v_ref[0, rows, lanes].astype(F32)

            def col(off):
                return jnp.sum(jnp.where(lane == head + off, blk, 0.0), axis=-1, keepdims=True)

            beta = col(0)
            gc = col(GDN_HEADS)
            glast = col(2 * GDN_HEADS)
            gc_row = gt_ref[0, pl.ds(head, 1), rows]
            eg = jnp.exp2(gc)
            kb = kn * beta
            rhs = jnp.concatenate([(vn * beta).astype(BF16), (kb * eg).astype(BF16)], axis=1)
            q_dec = qn * eg
            k_dec = (kn * jnp.exp2(glast - gc)).astype(BF16)
            kn_b = kn.astype(BF16)
            kk = _mm_nt(kb, kn_b)
            qk = _mm_nt(qn, kn_b)
            decay = jnp.exp2(jnp.where(same64 & tri, gc - gc_row, -1e30))
            lmat = (kk * decay).astype(BF16)
            intra = (qk * decay).astype(BF16)
            return lmat, intra, rhs, q_dec, k_dec, glast

        pre = [prep(hh, g) for hh, g in units]
        us = range(len(units))
        lm = [p[0] for p in pre]

        lm_s = [to_side(lm[i]) for i in us]
        d8_s = [lm_s[i] * side_masks[0] for i in us]
        d8_b = [lm[i] * level_masks[0] for i in us]
        d8_2s = [_mm(d8_s[i], d8_b[i]).astype(BF16) for i in us]
        d8_2b = [to_blockdiag(d8_2s[i]) for i in us]
        d8_4b = [to_blockdiag(_mm(d8_2s[i], d8_2b[i]).astype(BF16)) for i in us]
        x = [eye_s - d8_s[i].astype(F32) for i in us]
        x = [x[i] + _mm(x[i], d8_2b[i]) for i in us]
        x = [x[i] + _mm(x[i], d8_4b[i]) for i in us]
        for level in side_masks[1:]:
            xs = [x[i].astype(BF16) for i in us]
            xb = [to_blockdiag(xs[i]) for i in us]
            t = [to_blockdiag(_mm(lm_s[i] * level, xb[i]).astype(BF16)) for i in us]
            x = [x[i] - _mm(xs[i], t[i]) for i in us]
        x = [to_blockdiag(x[i].astype(BF16)) for i in us]

        uw = [_mm(x[i], pre[i][2]).astype(BF16) for i in us]
        iu = [_mm(pre[i][1], uw[i]) for i in us]
        for i, (h, g) in enumerate(units):
            _, _, _, q_dec, k_dec, glast = pre[i]
            rows = pl.ds(pl.multiple_of(g * GROUP, GROUP), GROUP)
            oq_scr[h, rows, :] = jnp.concatenate(
                [iu[i][:, :GDN_D], q_dec - iu[i][:, GDN_D:]], axis=1).astype(BF16)
            for c in range(cpg):
                cs = slice(c * CHUNK, (c + 1) * CHUNK)
                bm = _mm_tn(k_dec[cs], uw[i][cs])
                n = g * cpg + c
                mb_scr[h, n] = bm.astype(BF16)
                dl_scr[h, n] = jnp.broadcast_to(
                    jnp.exp2(glast[c * CHUNK:c * CHUNK + 8]), (8, LANES))

    nw = nw_ref[...]

    def chunk_step(n, states):
        rows = pl.ds(pl.multiple_of(n * CHUNK, CHUNK), CHUNK)
        new_states = []
        for hh in range(hb):
            lanes = slice(hh * LANES, (hh + 1) * LANES)
            s = states[hh]
            sb = s.astype(BF16)
            oq = oq_scr[hh, rows, :]
            mb = mb_scr[hh, n]
            o = (jnp.dot(oq[:, GDN_D:], sb, preferred_element_type=F32)
                 + oq[:, :GDN_D].astype(F32))
            ms = jnp.mean(o * o, axis=-1, keepdims=True)
            zz = z_ref[0, rows, lanes].astype(F32)
            o_ref[0, rows, lanes] = (o * lax.rsqrt(ms + EPS) * nw * _silu(zz)).astype(BF16)
            s = (s * dl_scr[hh, n][0:1, :]
                 - jnp.dot(mb[:, GDN_D:], sb, preferred_element_type=F32)
                 + mb[:, :GDN_D].astype(F32))
            new_states.append(s)
        return tuple(new_states)

    def recur(sg, states):
        for c in range(GROUPS_PER_STEP * cpg):
            states = chunk_step(sg * (GROUPS_PER_STEP * cpg) + c, states)
        return states

    def fused_step(sg, states):
        states = recur(sg - 1, states)
        group_step(sg)
        return states

    nsteps = ngroups // GROUPS_PER_STEP
    group_step(0)
    init = tuple(jnp.zeros((GDN_D, GDN_D), F32) for _ in range(hb))
    states = lax.fori_loop(1, nsteps, fused_step, init)
    recur(nsteps - 1, states)


def _gdn(u3, bgc3, gt, norm_w, hb):
    b, seq, _ = u3.shape
    w = hb * LANES
    nblk = GDN_HEADS // hb
    nchunks = seq // CHUNK
    body = functools.partial(_gdn_body, hb=hb, seq=seq)

    def tok(col0):
        return pl.BlockSpec((1, seq, w), lambda i, j: (i, 0, col0 + j))

    return pl.pallas_call(
        body,
        grid=(b, nblk),
        in_specs=[
            tok(0), tok(nblk), tok(2 * nblk), tok(3 * nblk),
            pl.BlockSpec((1, seq, LANES), lambda i, j: (i, 0, 0)),
            pl.BlockSpec((1, GDN_HEADS, seq), lambda i, j: (i, 0, 0)),
            pl.BlockSpec((1, GDN_D), lambda i, j: (0, 0)),
        ],
        out_specs=pl.BlockSpec((1, seq, w), lambda i, j: (i, 0, j)),
        out_shape=jax.ShapeDtypeStruct((b, seq, GDN_HEADS * GDN_D), BF16),
        scratch_shapes=[
            pltpu.VMEM((hb, nchunks, GDN_D, 2 * GDN_D), BF16),
            pltpu.VMEM((hb, seq, 2 * GDN_D), BF16),
            pltpu.VMEM((hb, nchunks, 8, LANES), F32),
        ],
        compiler_params=pltpu.CompilerParams(
            dimension_semantics=("arbitrary", "arbitrary"), vmem_limit_bytes=VMEM_LIMIT),
        name="gdn",
    )(u3, u3, u3, u3, bgc3, gt, norm_w)


def _attn_body(q_ref, k_ref, v_ref, c_ref, sa_ref, sb_ref, qnw_ref, knw_ref, lam_ref, sw_ref,
               o_ref, k_scr, q_scr, vt_scr, *, seq, ta, lambda_init):
    nt = seq // ta
    low = lax.broadcasted_iota(jnp.int32, (ta, LANES), 1) < DIFF_D

    lane = lax.broadcasted_iota(jnp.int32, (ta, LANES), 1)
    ri = lax.broadcasted_iota(jnp.int32, (LANES, LANES), 0)
    ci = lax.broadcasted_iota(jnp.int32, (LANES, LANES), 1)
    eye = (ri == ci).astype(BF16)
    same_half = ((ri < DIFF_D) == (ci < DIFF_D)).astype(BF16)
    nt_dims = (((1,), (1,)), ((), ()))

    def norm_rope(x, w, rows):
        ms = jnp.dot((x * x).astype(BF16), same_half, preferred_element_type=F32) * (1.0 / DIFF_D)
        y = x * lax.rsqrt(ms + EPS) * w
        half = ROPE_DIM // 2
        return (y * c_ref[rows, :] + pltpu.roll(y, LANES - half, axis=1) * sa_ref[rows, :]
                + pltpu.roll(y, half, axis=1) * sb_ref[rows, :])

    qscale = (DIFF_D ** -0.5) * math.log2(math.e)
    wq_max = jnp.max(jnp.abs(qnw_ref[...]), axis=-1, keepdims=True)
    wk_max = jnp.max(jnp.abs(knw_ref[...]), axis=-1, keepdims=True)
    shift = (BOUND_SLACK * DIFF_D * qscale) * wq_max * wk_max
    bound_ok = jnp.max(shift) <= MAX_SAFE_SHIFT
    s_hi = shift.astype(BF16).astype(F32)
    s_mid = (shift - s_hi).astype(BF16).astype(F32)
    s_lo = shift - s_hi - s_mid
    shift_lane = (DIFF_D, 0)

    def shift_terms(base):
        return jnp.where(lane == base, -s_hi,
                         jnp.where(lane == base + 1, -s_mid, jnp.where(lane == base + 2, -s_lo, 0.0)))

    def ones_lanes(base):
        return jnp.where((lane >= base) & (lane < base + 3), 1.0, 0.0)

    def prepare():
        for t in range(nt):
            rows = slice(t * ta, (t + 1) * ta)
            kx = norm_rope(k_ref[0, rows, :].astype(F32), knw_ref[...], rows)
            k_scr[0, rows, :] = jnp.where(low, kx, ones_lanes(shift_lane[0])).astype(BF16)
            k_scr[1, rows, :] = jnp.where(low, ones_lanes(shift_lane[1]), kx).astype(BF16)
            qx = norm_rope(q_ref[0, rows, :].astype(F32), qnw_ref[...], rows) * qscale
            q_scr[0, rows, :] = jnp.where(low, qx, shift_terms(shift_lane[0])).astype(BF16)
            q_scr[1, rows, :] = jnp.where(low, shift_terms(shift_lane[1]), qx).astype(BF16)
            vt_scr[0:LANES, rows] = lax.dot_general(eye, v_ref[0, rows, :], nt_dims,
                                                    preferred_element_type=F32).astype(BF16)
        vt_scr[LANES:, :] = jnp.ones((ONES_ROWS, seq), BF16)

    lam = (jnp.exp(jnp.sum(lam_ref[0:1, :] * lam_ref[1:2, :], axis=-1, keepdims=True))
           - jnp.exp(jnp.sum(lam_ref[2:3, :] * lam_ref[3:4, :], axis=-1, keepdims=True))
           + lambda_init)
    krow = lax.broadcasted_iota(jnp.int32, (ta, ta), 0)
    qcol = lax.broadcasted_iota(jnp.int32, (ta, ta), 1)
    causal = krow <= qcol

    def scores(qi):
        d0 = qi * ta
        out = []
        for mp in range(2):
            qz = q_scr[mp, d0:d0 + ta, :]
            diag = lax.dot_general(k_scr[mp, d0:d0 + ta, :], qz, nt_dims,
                                   preferred_element_type=F32)
            diag = jnp.where(causal, diag, -1e30)
            past = None
            if qi > 0:
                past = lax.dot_general(k_scr[mp, 0:d0, :], qz, nt_dims,
                                       preferred_element_type=F32)
            out.append((past, diag))
        return out

    def finish(qi, sc, exact_max):
        d0 = qi * ta
        outs = []
        for past, diag in sc:
            if exact_max:
                m = jnp.max(diag, axis=0, keepdims=True)
                if past is not None:
                    m = jnp.maximum(m, jnp.max(past, axis=0, keepdims=True))
                diag = diag - m
                if past is not None:
                    past = past - m
            acc = jnp.dot(vt_scr[:, d0:d0 + ta], jnp.exp2(diag).astype(BF16),
                          preferred_element_type=F32)
            if past is not None:
                acc = acc + jnp.dot(vt_scr[:, 0:d0], jnp.exp2(past).astype(BF16),
                                    preferred_element_type=F32)
            outs.append(acc[0:LANES] / acc[LANES:LANES + 1])
        o = (outs[0] - lam * outs[1]).T
        ms = jnp.mean(o * o, axis=-1, keepdims=True)
        o_ref[0, d0:d0 + ta, :] = (o * lax.rsqrt(ms + EPS) * sw_ref[...]
                                   * (1.0 - lambda_init)).astype(BF16)

    def run(exact_max):
        prepare()
        sc = scores(0)
        for qi in range(nt):
            nxt = scores(qi + 1) if qi + 1 < nt else None
            finish(qi, sc, exact_max)
            sc = nxt

    @pl.when(bound_ok)
    def _():
        run(False)

    @pl.when(jnp.logical_not(bound_ok))
    def _():
        run(True)


def _attn(u3, rope_tabs, qnw, knw, lam4, subln_w, lambda_init, ta):
    b, seq, _ = u3.shape
    c_t, sa_t, sb_t = rope_tabs
    nt = seq // ta
    body = functools.partial(_attn_body, seq=seq, ta=ta, lambda_init=lambda_init)
    qcol, kcol, vcol = 4 * GDN_HEADS, 4 * GDN_HEADS + DIFF_HEADS, 4 * GDN_HEADS + 2 * DIFF_HEADS
    tab = pl.BlockSpec((seq, LANES), lambda i, h: (0, 0))
    vec = pl.BlockSpec((1, LANES), lambda i, h: (0, 0))

    def tok(col0):
        return pl.BlockSpec((1, seq, LANES), lambda i, h: (i, 0, col0 + h))

    return pl.pallas_call(
        body,
        grid=(b, DIFF_HEADS),
        in_specs=[tok(qcol), tok(kcol), tok(vcol), tab, tab, tab, vec, vec,
                  pl.BlockSpec((4, LANES), lambda i, h: (0, 0)), vec],
        out_specs=pl.BlockSpec((1, seq, LANES), lambda i, h: (i, 0, h)),
        out_shape=jax.ShapeDtypeStruct((b, seq, DIFF_HEADS * 2 * DIFF_D), BF16),
        scratch_shapes=[
            pltpu.VMEM((2, seq, LANES), BF16),
            pltpu.VMEM((2, seq, LANES), BF16),
            pltpu.VMEM((LANES + ONES_ROWS, seq), BF16),
        ],
        compiler_params=pltpu.CompilerParams(
            dimension_semantics=("arbitrary", "arbitrary"), vmem_limit_bytes=VMEM_LIMIT),
        name="attn",
    )(u3, u3, u3, c_t, sa_t, sb_t, qnw, knw, lam4, subln_w)


def _mixer_body(x_ref, og_ref, od_ref, ga_ref, gb_ref, bg_ref, wg_ref, wd_ref, wo_ref, o_ref):
    d = x_ref.shape[1]
    ya = jnp.dot(og_ref[...], wg_ref[...], preferred_element_type=F32)
    yb = jnp.dot(od_ref[...], wd_ref[...], preferred_element_type=F32)
    ga = _sigmoid(ga_ref[...].astype(F32) + bg_ref[:, :d])
    gb = _sigmoid(gb_ref[...].astype(F32) + bg_ref[:, d:])
    mix = (ga * ya + gb * yb).astype(BF16)
    o_ref[...] = x_ref[...] + jnp.dot(mix, wo_ref[...], preferred_element_type=F32)


def _mixer(x2d, og, od, u, b_gate, wg, wd, wo, gate_col, tm):
    m, d = x2d.shape
    row = lambda i: (i, 0)
    full = lambda i: (0, 0)
    return pl.pallas_call(
        _mixer_body,
        grid=(m // tm,),
        in_specs=[
            pl.BlockSpec((tm, d), row),
            pl.BlockSpec((tm, d), row),
            pl.BlockSpec((tm, d), row),
            pl.BlockSpec((tm, d), lambda i: (i, gate_col)),
            pl.BlockSpec((tm, d), lambda i: (i, gate_col + 1)),
            pl.BlockSpec((1, 2 * d), full),
            pl.BlockSpec((d, d), full),
            pl.BlockSpec((d, d), full),
            pl.BlockSpec((d, d), full),
        ],
        out_specs=pl.BlockSpec((tm, d), row),
        out_shape=jax.ShapeDtypeStruct((m, d), F32),
        compiler_params=pltpu.CompilerParams(
            dimension_semantics=("arbitrary",), vmem_limit_bytes=VMEM_LIMIT),
        name="mixer",
    )(x2d, og, od, u, u, b_gate, wg, wd, wo)


def _ffn_body(x_ref, n2_ref, wup_ref, cw_ref, cb_ref, wdn_ref, o_ref, carry_scr, *, tpb, tf):
    i = pl.program_id(0)
    dff = wdn_ref.shape[0]
    x = x_ref[...]
    ms = jnp.mean(x * x, axis=-1, keepdims=True)
    hb = (x * lax.rsqrt(ms + EPS) * n2_ref[...]).astype(BF16)
    tm = x.shape[0]
    first = (i % tpb) == 0

    def conv(u, cols):
        prev = jnp.where(first, 0.0, carry_scr[:, cols])
        w = cw_ref[:, cols]
        acc = u * w[FFN_CONV - 1:FFN_CONV] + cb_ref[:, cols]
        for s in range(1, FFN_CONV):
            acc = acc + _shift_rows(u, prev, s) * w[FFN_CONV - 1 - s:FFN_CONV - s]
        carry_scr[:, cols] = u[tm - 8:tm]
        return acc

    acc = x
    for f in range(dff // tf):
        gcols = slice(f * tf, (f + 1) * tf)
        vcols = slice(dff + f * tf, dff + (f + 1) * tf)
        ug = jnp.dot(hb, wup_ref[:, gcols], preferred_element_type=F32)
        uv = jnp.dot(hb, wup_ref[:, vcols], preferred_element_type=F32)
        act = (_silu(conv(ug, gcols)) * conv(uv, vcols)).astype(BF16)
        acc = acc + jnp.dot(act, wdn_ref[gcols, :], preferred_element_type=F32)
    o_ref[...] = acc


def _ffn(x2d, n2, wup, cw, cb, wdn, seq, tm, tf):
    m, d = x2d.shape
    dff = wdn.shape[0]
    full = lambda i: (0, 0)
    body = functools.partial(_ffn_body, tpb=seq // tm, tf=tf)
    return pl.pallas_call(
        body,
        grid=(m // tm,),
        in_specs=[
            pl.BlockSpec((tm, d), lambda i: (i, 0)),
            pl.BlockSpec((1, d), full),
            pl.BlockSpec((d, 2 * dff), full, pipeline_mode=pl.Buffered(1)),
            pl.BlockSpec((FFN_CONV, 2 * dff), full),
            pl.BlockSpec((1, 2 * dff), full),
            pl.BlockSpec((dff, d), full, pipeline_mode=pl.Buffered(1)),
        ],
        out_specs=pl.BlockSpec((tm, d), lambda i: (i, 0)),
        out_shape=jax.ShapeDtypeStruct((m, d), F32),
        scratch_shapes=[pltpu.VMEM((8, 2 * dff), F32)],
        compiler_params=pltpu.CompilerParams(
            dimension_semantics=("arbitrary",), vmem_limit_bytes=VMEM_LIMIT),
        name="ffn",
    )(x2d, n2, wup, cw, cb, wdn)


def _rope_tables(seq):
    half = ROPE_DIM // 2
    pos = jnp.arange(seq, dtype=F32)
    inv_freq = ROPE_THETA ** (-jnp.arange(0, ROPE_DIM, 2, dtype=F32) / ROPE_DIM)
    ang = pos[:, None] * inv_freq[None, :]
    cos, sin = jnp.cos(ang), jnp.sin(ang)
    ones = jnp.ones((seq, DIFF_D - ROPE_DIM), F32)
    zeros = jnp.zeros((seq, DIFF_D - ROPE_DIM), F32)
    zh = jnp.zeros((seq, half), F32)
    c = jnp.concatenate([cos, cos, ones], axis=-1)
    sa = jnp.concatenate([-sin, zh, zeros], axis=-1)
    sb = jnp.concatenate([zh, sin, zeros], axis=-1)
    return tuple(jnp.concatenate([t, t], axis=-1) for t in (c, sa, sb))


def _pad_lanes(v, offset):
    return jnp.zeros((LANES,), F32).at[offset:offset + v.shape[0]].set(v.astype(F32))


def kernel(x, norm1_w, w_in, b_gate, gdn_conv_w, gdn_A_log, gdn_dt_bias, gdn_norm_w,
           diff_q_norm_w, diff_k_norm_w, lambda_q1, lambda_k1, lambda_q2, lambda_k2,
           diff_subln_w, w_gdn_out, w_diff_out, w_o, norm2_w, w_up, ffn_conv_w,
           ffn_conv_b, w_down):
    b, seq, d = x.shape
    depth = norm1_w.shape[0]
    gqk = GDN_HEADS * GDN_D
    dqk = DIFF_HEADS * 2 * DIFF_D
    o_b = 4 * gqk
    o_dq = o_b + 2 * GDN_HEADS
    o_gate = o_dq + 3 * dqk
    dff = w_down.shape[1]
    rope_tabs = _rope_tables(seq)
    x2d = x.reshape(b * seq, d)
    tm_in = min(2048, seq)
    tq = min(256, seq)
    tm_mix = min(512, seq)
    tm_ffn = min(512, seq)
    tf = dff

    for layer in range(depth):
        lambda_init = 0.8 - 0.6 * math.exp(-0.3 * layer)
        wl = w_in[layer]
        w_main = jnp.concatenate([wl[:, :o_b], wl[:, o_dq:]], axis=1).astype(BF16)
        w_ba = jnp.pad(wl[:, o_b:o_dq], ((0, 0), (0, LANES - 2 * GDN_HEADS))).astype(BF16)
        gp = jnp.stack([_pad_lanes(gdn_A_log[layer], GDN_HEADS),
                        _pad_lanes(gdn_dt_bias[layer], GDN_HEADS)])
        u, bgc, gt = _inproj(x2d, norm1_w[layer][None, :], w_main, w_ba, gp, gdn_conv_w[layer],
                              seq, tm_in, 1024)
        u3 = u.reshape(b, seq, u.shape[1])
        og = _gdn(u3, bgc.reshape(b, seq, LANES), gt,
                  gdn_norm_w[layer][None, :], hb=4)
        qnw = jnp.tile(diff_q_norm_w[layer], 2)[None, :]
        knw = jnp.tile(diff_k_norm_w[layer], 2)[None, :]
        lam4 = jnp.stack([jnp.pad(v[layer], (0, LANES - DIFF_D))
                          for v in (lambda_q1, lambda_k1, lambda_q2, lambda_k2)])
        od = _attn(u3, rope_tabs, qnw, knw, lam4, diff_subln_w[layer][None, :], lambda_init, tq)
        x2d = _mixer(x2d, og.reshape(b * seq, gqk), od.reshape(b * seq, dqk), u,
                     b_gate[layer][None, :], w_gdn_out[layer].astype(BF16),
                     w_diff_out[layer].astype(BF16), w_o[layer].astype(BF16),
                     (4 * gqk + 3 * dqk) // d, tm_mix)
        x2d = _ffn(x2d, norm2_w[layer][None, :], w_up[layer].astype(BF16), ffn_conv_w[layer],
                   ffn_conv_b[layer][None, :], w_down[layer].astype(BF16), seq, tm_ffn, tf)
    return x2d.reshape(b, seq, d)
```

```python
import functools
import math

import jax
import jax.numpy as jnp
from jax import lax
from jax.experimental import pallas as pl
from jax.experimental.pallas import tpu as pltpu

F32 = jnp.float32
BF16 = jnp.bfloat16

EPS = 1e-6
LANES = 128
GDN_HEADS = 8
GDN_D = 128
GDN_CONV = 4
CHUNK = 64
GROUP = 256
GROUPS_PER_STEP = 2
INPROJ_STEPS = 6
PROJ_ROWS = 512
DIFF_HEADS = 8
DIFF_D = 64
ROPE_DIM = DIFF_D // 4
ROPE_THETA = 500000.0
FFN_CONV = 3
ONES_ROWS = 16
BOUND_SLACK = 1.01
MAX_SAFE_SHIFT = 50.0
LOG2E = math.log2(math.e)
VMEM_LIMIT = 56 * 1024 * 1024


def _mm(a, b):
    return jnp.dot(a.astype(BF16), b.astype(BF16), preferred_element_type=F32)


def _mm_nt(a, b):
    return lax.dot_general(a.astype(BF16), b.astype(BF16), (((1,), (1,)), ((), ())),
                           preferred_element_type=F32)


def _mm_tn(a, b):
    return lax.dot_general(a.astype(BF16), b.astype(BF16), (((0,), (0,)), ((), ())),
                           preferred_element_type=F32)


def _sigmoid(x):
    return 0.5 * jnp.tanh(0.5 * x) + 0.5


def _silu(x):
    h = 0.5 * x
    return h + h * jnp.tanh(h)


def _shift_rows(cur, prev8, s):
    tiles = cur.reshape(cur.shape[0] // 8, 8, cur.shape[1])
    rot = pltpu.roll(tiles, s, axis=1)
    before = jnp.concatenate([pltpu.roll(prev8, s, axis=0)[None], rot[:-1]], axis=0)
    row = lax.broadcasted_iota(jnp.int32, tiles.shape, 1)
    return jnp.where(row < s, before, rot).reshape(cur.shape)


def _inproj_body(x_ref, n1_ref, wc_ref, wp_ref, wba_ref, gp_ref, cw_ref,
                 uc_ref, up_ref, bgc_ref, gt_ref, h_scr):
    j = pl.program_id(1)

    @pl.when(j == 0)
    def _():
        x = x_ref[...]
        ms = jnp.mean(x * x, axis=-1, keepdims=True)
        hb = (x * lax.rsqrt(ms + EPS) * n1_ref[...]).astype(BF16)
        h_scr[...] = hb
        ba = jnp.dot(hb, wba_ref[...], preferred_element_type=F32)
        tm = ba.shape[0]
        lane = lax.broadcasted_iota(jnp.int32, ba.shape, 1)
        row = lax.broadcasted_iota(jnp.int32, ba.shape, 0)
        beta = _sigmoid(ba)
        zz = ba + gp_ref[1:2, :]
        softplus = jnp.maximum(zz, 0.0) + jnp.log1p(jnp.exp(-jnp.abs(zz)))
        g = -(jnp.exp(gp_ref[0:1, :]) * LOG2E) * softplus
        rin = row & (CHUNK - 1)
        gc = g
        s = 1
        while s < CHUNK:
            gc = gc + jnp.where(rin >= s, pltpu.roll(gc, s, axis=0), 0.0)
            s *= 2
        glast = jnp.broadcast_to(
            gc.reshape(tm // CHUNK, CHUNK, LANES)[:, CHUNK - 1:CHUNK, :],
            (tm // CHUNK, CHUNK, LANES)).reshape(tm, LANES)
        glast = pltpu.roll(glast, GDN_HEADS, axis=1)
        out = jnp.where(lane < GDN_HEADS, beta,
                        jnp.where(lane < 2 * GDN_HEADS, gc,
                                  jnp.where(lane < 3 * GDN_HEADS, glast, 0.0)))
        bgc_ref[...] = out
        gt_ref[0] = out.T[GDN_HEADS:2 * GDN_HEADS, :]

    tm = h_scr.shape[0]
    nch = tm // PROJ_ROWS

    def chunk_dots(c):
        hc = h_scr[c * PROJ_ROWS:(c + 1) * PROJ_ROWS, :]
        return (jnp.dot(hc, wc_ref[...], preferred_element_type=F32),
                jnp.dot(hc, wp_ref[...], preferred_element_type=F32))

    def conv_silu(cur, prev):
        w = 0.5 * cw_ref[...]
        x2 = _shift_rows(cur, prev, 2)
        odd = cur * w[2:3] + x2 * w[0:1]
        odd_prev = prev * w[2:3] + pltpu.roll(prev, 2, axis=0) * w[0:1]
        h = cur * w[3:4] + x2 * w[1:2] + _shift_rows(odd, odd_prev, 1)
        return h + h * jnp.tanh(h)

    nxt = chunk_dots(0)
    prev = jnp.zeros((8, wc_ref.shape[1]), F32)
    for c in range(nch):
        cur_c, cur_p = nxt
        if c + 1 < nch:
            nxt = chunk_dots(c + 1)
        rows = slice(c * PROJ_ROWS, (c + 1) * PROJ_ROWS)
        uc_ref[rows, :] = conv_silu(cur_c, prev).astype(BF16)
        up_ref[rows, :] = cur_p.astype(BF16)
        prev = cur_c[PROJ_ROWS - 8:PROJ_ROWS]


def _inproj(x2d, n1, w_conv, w_plain, wba, gp, conv_w, seq, tm, nsteps):
    m, d = x2d.shape
    assert tm == seq
    tc = w_conv.shape[1] // nsteps
    tp = w_plain.shape[1] // nsteps
    tpb = seq // tm
    return pl.pallas_call(
        _inproj_body,
        grid=(m // tm, nsteps),
        in_specs=[
            pl.BlockSpec((tm, d), lambda i, j: (i, 0)),
            pl.BlockSpec((1, d), lambda i, j: (0, 0)),
            pl.BlockSpec((d, tc), lambda i, j: (0, j)),
            pl.BlockSpec((d, tp), lambda i, j: (0, j)),
            pl.BlockSpec((d, LANES), lambda i, j: (0, 0)),
            pl.BlockSpec((2, LANES), lambda i, j: (0, 0)),
            pl.BlockSpec((GDN_CONV, tc), lambda i, j: (0, j)),
        ],
        out_specs=[
            pl.BlockSpec((tm, tc), lambda i, j: (i, j)),
            pl.BlockSpec((tm, tp), lambda i, j: (i, j)),
            pl.BlockSpec((tm, LANES), lambda i, j: (i, 0)),
            pl.BlockSpec((1, GDN_HEADS, tm), lambda i, j: (i // tpb, 0, i % tpb)),
        ],
        out_shape=[
            jax.ShapeDtypeStruct((m, w_conv.shape[1]), BF16),
            jax.ShapeDtypeStruct((m, w_plain.shape[1]), BF16),
            jax.ShapeDtypeStruct((m, LANES), F32),
            jax.ShapeDtypeStruct((m // seq, GDN_HEADS, seq), F32),
        ],
        scratch_shapes=[pltpu.VMEM((tm, d), BF16)],
        compiler_params=pltpu.CompilerParams(
            dimension_semantics=("arbitrary", "arbitrary"), vmem_limit_bytes=VMEM_LIMIT),
        name="inproj",
    )(x2d, n1, w_conv, w_plain, wba, gp, conv_w)


def _gdn_body(q_ref, k_ref, v_ref, z_ref, bgc_ref, gt_ref, nw_ref,
              o_ref, mb_scr, oq_scr, dl_scr, *, hb, seq):
    hbase = pl.program_id(1) * hb
    ngroups = seq // GROUP
    nchunks = seq // CHUNK
    cpg = GROUP // CHUNK

    ri = lax.broadcasted_iota(jnp.int32, (GROUP, GROUP), 0)
    ci = lax.broadcasted_iota(jnp.int32, (GROUP, GROUP), 1)
    tri = ci <= ri
    same64 = (ri >> 6) == (ci >> 6)

    def selectors(r, c):
        def same(shift):
            return (r >> shift) == (c >> shift)
        first = (c < r) & same(3)
        return [first.astype(BF16)] + [
            ((c < r) & same(s + 1) & jnp.logical_not(same(s))).astype(BF16) for s in (3, 4, 5)]

    rs = lax.broadcasted_iota(jnp.int32, (CHUNK, GROUP), 0)
    cs = lax.broadcasted_iota(jnp.int32, (CHUNK, GROUP), 1)
    level_masks = selectors(ri, ci)
    side_masks = selectors(rs, cs & (CHUNK - 1))
    eye_s = (rs == (cs & (CHUNK - 1))).astype(F32)
    chunk_cols = [((cs >> 6) == c).astype(BF16) for c in range(cpg)]
    lane = lax.broadcasted_iota(jnp.int32, (GROUP, LANES), 1)

    def to_side(a):
        out = a[0:CHUNK]
        for c in range(1, cpg):
            out = out + a[c * CHUNK:(c + 1) * CHUNK]
        return out

    def to_blockdiag(a):
        return jnp.concatenate([a * chunk_cols[c] for c in range(cpg)], axis=0)

    def l2n(x, scale=1.0):
        return x * (lax.rsqrt(jnp.sum(x * x, axis=-1, keepdims=True) + EPS) * scale)

    def group_step(sg):
        units = [(hh, sg * GROUPS_PER_STEP + sub) for sub in range(GROUPS_PER_STEP)
                 for hh in range(hb)]

        def prep(hh, g):
            r0 = pl.multiple_of(g * GROUP, GROUP)
            rows = pl.ds(r0, GROUP)
            blk = bgc_ref[0, rows, :]
            lanes = slice(hh * LANES, (hh + 1) * LANES)
            head = hbase + hh
            qn = l2n(q_ref[0, rows, lanes].astype(F32), GDN_D ** -0.5)
            kn = l2n(k_ref[0, rows, lanes].astype(F32))
            vn = v_ref[0, rows, lanes].astype(F32)

            def col(off):
                return jnp.sum(jnp.where(lane == head + off, blk, 0.0), axis=-1, keepdims=True)

            beta = col(0)
            gc = col(GDN_HEADS)
            glast = col(2 * GDN_HEADS)
            gc_row = gt_ref[0, pl.ds(head, 1), rows]
            eg = jnp.exp2(gc)
            kb = kn * beta
            rhs = jnp.concatenate([(vn * beta).astype(BF16), (kb * eg).astype(BF16)], axis=1)
            q_dec = qn * eg
            k_dec = (kn * jnp.exp2(glast - gc)).astype(BF16)
            kn_b = kn.astype(BF16)
            kk = _mm_nt(kb, kn_b)
            qk = _mm_nt(qn, kn_b)
            decay = jnp.exp2(jnp.where(same64 & tri, gc - gc_row, -1e30))
            lmat = (kk * decay).astype(BF16)
            intra = (qk * decay).astype(BF16)
            return lmat, intra, rhs, q_dec, k_dec, glast

        pre = [prep(hh, g) for hh, g in units]
        us = range(len(units))
        lm = [p[0] for p in pre]

        lm_s = [to_side(lm[i]) for i in us]
        d8_s = [lm_s[i] * side_masks[0] for i in us]
        d8_b = [lm[i] * level_masks[0] for i in us]
        d8_2s = [_mm(d8_s[i], d8_b[i]).astype(BF16) for i in us]
        d8_2b = [to_blockdiag(d8_2s[i]) for i in us]
        d8_4b = [to_blockdiag(_mm(d8_2s[i], d8_2b[i]).astype(BF16)) for i in us]
        x = [eye_s - d8_s[i].astype(F32) for i in us]
        x = [x[i] + _mm(x[i], d8_2b[i]) for i in us]
        x = [x[i] + _mm(x[i], d8_4b[i]) for i in us]
        for level in side_masks[1:]:
            xs = [x[i].astype(BF16) for i in us]
            xb = [to_blockdiag(xs[i]) for i in us]
            t = [to_blockdiag(_mm(lm_s[i] * level, xb[i]).astype(BF16)) for i in us]
            x = [x[i] - _mm(xs[i], t[i]) for i in us]
        x = [to_blockdiag(x[i].astype(BF16)) for i in us]

        uw = [_mm(x[i], pre[i][2]).astype(BF16) for i in us]
        iu = [_mm(pre[i][1], uw[i]) for i in us]
        for i, (h, g) in enumerate(units):
            _, _, _, q_dec, k_dec, glast = pre[i]
            rows = pl.ds(pl.multiple_of(g * GROUP, GROUP), GROUP)
            oq_scr[h, rows, :] = jnp.concatenate(
                [iu[i][:, :GDN_D], q_dec - iu[i][:, GDN_D:]], axis=1).astype(BF16)
            for c in range(cpg):
                cs = slice(c * CHUNK, (c + 1) * CHUNK)
                bm = _mm_tn(k_dec[cs], uw[i][cs])
                n = g * cpg + c
                mb_scr[h, n] = bm.astype(BF16)
                dl_scr[h, n] = jnp.broadcast_to(
                    jnp.exp2(glast[c * CHUNK:c * CHUNK + 8]), (8, LANES))

    nw = nw_ref[...]

    def chunk_step(n, states):
        rows = pl.ds(pl.multiple_of(n * CHUNK, CHUNK), CHUNK)
        new_states = []
        for hh in range(hb):
            lanes = slice(hh * LANES, (hh + 1) * LANES)
            s = states[hh]
            sb = s.astype(BF16)
            oq = oq_scr[hh, rows, :]
            mb = mb_scr[hh, n]
            o = (jnp.dot(oq[:, GDN_D:], sb, preferred_element_type=F32)
                 + oq[:, :GDN_D].astype(F32))
            ms = jnp.mean(o * o, axis=-1, keepdims=True)
            zz = z_ref[0, rows, lanes].astype(F32)
            o_ref[0, rows, lanes] = (o * lax.rsqrt(ms + EPS) * nw * _silu(zz)).astype(BF16)
            s = (s * dl_scr[hh, n][0:1, :]
                 - jnp.dot(mb[:, GDN_D:], sb, preferred_element_type=F32)
                 + mb[:, :GDN_D].astype(F32))
            new_states.append(s)
        return tuple(new_states)

    def recur(sg, states):
        for c in range(GROUPS_PER_STEP * cpg):
            states = chunk_step(sg * (GROUPS_PER_STEP * cpg) + c, states)
        return states

    def fused_step(sg, states):
        states = recur(sg - 1, states)
        group_step(sg)
        return states

    nsteps = ngroups // GROUPS_PER_STEP
    group_step(0)
    init = tuple(jnp.zeros((GDN_D, GDN_D), F32) for _ in range(hb))
    states = lax.fori_loop(1, nsteps, fused_step, init)
    recur(nsteps - 1, states)


def _gdn(uc3, up3, bgc3, gt, norm_w, hb):
    b, seq, _ = uc3.shape
    w = hb * LANES
    nblk = GDN_HEADS // hb
    nchunks = seq // CHUNK
    body = functools.partial(_gdn_body, hb=hb, seq=seq)

    def tok(col0):
        return pl.BlockSpec((1, seq, w), lambda i, j: (i, 0, col0 + j))

    return pl.pallas_call(
        body,
        grid=(b, nblk),
        in_specs=[
            tok(0), tok(nblk), tok(2 * nblk), tok(0),
            pl.BlockSpec((1, seq, LANES), lambda i, j: (i, 0, 0)),
            pl.BlockSpec((1, GDN_HEADS, seq), lambda i, j: (i, 0, 0)),
            pl.BlockSpec((1, GDN_D), lambda i, j: (0, 0)),
        ],
        out_specs=pl.BlockSpec((1, seq, w), lambda i, j: (i, 0, j)),
        out_shape=jax.ShapeDtypeStruct((b, seq, GDN_HEADS * GDN_D), BF16),
        scratch_shapes=[
            pltpu.VMEM((hb, nchunks, GDN_D, 2 * GDN_D), BF16),
            pltpu.VMEM((hb, seq, 2 * GDN_D), BF16),
            pltpu.VMEM((hb, nchunks, 8, LANES), F32),
        ],
        compiler_params=pltpu.CompilerParams(
            dimension_semantics=("arbitrary", "arbitrary"), vmem_limit_bytes=VMEM_LIMIT),
        name="gdn",
    )(uc3, uc3, uc3, up3, bgc3, gt, norm_w)


def _attn_body(q_ref, k_ref, v_ref, c_ref, sa_ref, sb_ref, qnw_ref, knw_ref, lam_ref, sw_ref,
               o_ref, k_scr, q_scr, vt_scr, *, seq, ta, lambda_init):
    nt = seq // ta
    low = lax.broadcasted_iota(jnp.int32, (ta, LANES), 1) < DIFF_D

    lane = lax.broadcasted_iota(jnp.int32, (ta, LANES), 1)
    ri = lax.broadcasted_iota(jnp.int32, (LANES, LANES), 0)
    ci = lax.broadcasted_iota(jnp.int32, (LANES, LANES), 1)
    eye = (ri == ci).astype(BF16)
    same_half = ((ri < DIFF_D) == (ci < DIFF_D)).astype(BF16)
    nt_dims = (((1,), (1,)), ((), ()))

    def norm_rope(x, w, rows):
        ms = jnp.dot((x * x).astype(BF16), same_half, preferred_element_type=F32) * (1.0 / DIFF_D)
        y = x * lax.rsqrt(ms + EPS) * w
        half = ROPE_DIM // 2
        return (y * c_ref[rows, :] + pltpu.roll(y, LANES - half, axis=1) * sa_ref[rows, :]
                + pltpu.roll(y, half, axis=1) * sb_ref[rows, :])

    qscale = (DIFF_D ** -0.5) * math.log2(math.e)
    wq_max = jnp.max(jnp.abs(qnw_ref[...]), axis=-1, keepdims=True)
    wk_max = jnp.max(jnp.abs(knw_ref[...]), axis=-1, keepdims=True)
    shift = (BOUND_SLACK * DIFF_D * qscale) * wq_max * wk_max
    bound_ok = jnp.max(shift) <= MAX_SAFE_SHIFT
    s_hi = shift.astype(BF16).astype(F32)
    s_mid = (shift - s_hi).astype(BF16).astype(F32)
    s_lo = shift - s_hi - s_mid
    shift_lane = (DIFF_D, 0)

    def shift_terms(base):
        return jnp.where(lane == base, -s_hi,
                         jnp.where(lane == base + 1, -s_mid, jnp.where(lane == base + 2, -s_lo, 0.0)))

    def ones_lanes(base):
        return jnp.where((lane >= base) & (lane < base + 3), 1.0, 0.0)

    def prepare():
        for t in range(nt):
            rows = slice(t * ta, (t + 1) * ta)
            kx = norm_rope(k_ref[0, rows, :].astype(F32), knw_ref[...], rows)
            k_scr[0, rows, :] = jnp.where(low, kx, ones_lanes(shift_lane[0])).astype(BF16)
            k_scr[1, rows, :] = jnp.where(low, ones_lanes(shift_lane[1]), kx).astype(BF16)
            qx = norm_rope(q_ref[0, rows, :].astype(F32), qnw_ref[...], rows) * qscale
            q_scr[0, rows, :] = jnp.where(low, qx, shift_terms(shift_lane[0])).astype(BF16)
            q_scr[1, rows, :] = jnp.where(low, shift_terms(shift_lane[1]), qx).astype(BF16)
            vt_scr[0:LANES, rows] = lax.dot_general(eye, v_ref[0, rows, :], nt_dims,
                                                    preferred_element_type=F32).astype(BF16)
        vt_scr[LANES:, :] = jnp.ones((ONES_ROWS, seq), BF16)

    lam = (jnp.exp(jnp.sum(lam_ref[0:1, :] * lam_ref[1:2, :], axis=-1, keepdims=True))
           - jnp.exp(jnp.sum(lam_ref[2:3, :] * lam_ref[3:4, :], axis=-1, keepdims=True))
           + lambda_init)
    krow = lax.broadcasted_iota(jnp.int32, (ta, ta), 0)
    qcol = lax.broadcasted_iota(jnp.int32, (ta, ta), 1)
    causal = krow <= qcol

    def scores(qi):
        d0 = qi * ta
        out = []
        for mp in range(2):
            qz = q_scr[mp, d0:d0 + ta, :]
            diag = lax.dot_general(k_scr[mp, d0:d0 + ta, :], qz, nt_dims,
                                   preferred_element_type=F32)
            diag = jnp.where(causal, diag, -1e30)
            past = None
            if qi > 0:
                past = lax.dot_general(k_scr[mp, 0:d0, :], qz, nt_dims,
                                       preferred_element_type=F32)
            out.append((past, diag))
        return out

    def finish(qi, sc, exact_max):
        d0 = qi * ta
        outs = []
        for past, diag in sc:
            if exact_max:
                m = jnp.max(diag, axis=0, keepdims=True)
                if past is not None:
                    m = jnp.maximum(m, jnp.max(past, axis=0, keepdims=True))
                diag = diag - m
                if past is not None:
                    past = past - m
            acc = jnp.dot(vt_scr[:, d0:d0 + ta], jnp.exp2(diag).astype(BF16),
                          preferred_element_type=F32)
            if past is not None:
                acc = acc + jnp.dot(vt_scr[:, 0:d0], jnp.exp2(past).astype(BF16),
                                    preferred_element_type=F32)
            outs.append(acc[0:LANES] / acc[LANES:LANES + 1])
        o = (outs[0] - lam * outs[1]).T
        ms = jnp.mean(o * o, axis=-1, keepdims=True)
        o_ref[0, d0:d0 + ta, :] = (o * lax.rsqrt(ms + EPS) * sw_ref[...]
                                   * (1.0 - lambda_init)).astype(BF16)

    def run(exact_max):
        prepare()
        sc = scores(0)
        for qi in range(nt):
            nxt = scores(qi + 1) if qi + 1 < nt else None
            finish(qi, sc, exact_max)
            sc = nxt

    @pl.when(bound_ok)
    def _():
        run(False)

    @pl.when(jnp.logical_not(bound_ok))
    def _():
        run(True)


def _attn(u3, rope_tabs, qnw, knw, lam4, subln_w, lambda_init, ta):
    b, seq, _ = u3.shape
    c_t, sa_t, sb_t = rope_tabs
    nt = seq // ta
    body = functools.partial(_attn_body, seq=seq, ta=ta, lambda_init=lambda_init)
    qcol, kcol, vcol = GDN_HEADS, GDN_HEADS + DIFF_HEADS, GDN_HEADS + 2 * DIFF_HEADS
    tab = pl.BlockSpec((seq, LANES), lambda i, h: (0, 0))
    vec = pl.BlockSpec((1, LANES), lambda i, h: (0, 0))

    def tok(col0):
        return pl.BlockSpec((1, seq, LANES), lambda i, h: (i, 0, col0 + h))

    return pl.pallas_call(
        body,
        grid=(b, DIFF_HEADS),
        in_specs=[tok(qcol), tok(kcol), tok(vcol), tab, tab, tab, vec, vec,
                  pl.BlockSpec((4, LANES), lambda i, h: (0, 0)), vec],
        out_specs=pl.BlockSpec((1, seq, LANES), lambda i, h: (i, 0, h)),
        out_shape=jax.ShapeDtypeStruct((b, seq, DIFF_HEADS * 2 * DIFF_D), BF16),
        scratch_shapes=[
            pltpu.VMEM((2, seq, LANES), BF16),
            pltpu.VMEM((2, seq, LANES), BF16),
            pltpu.VMEM((LANES + ONES_ROWS, seq), BF16),
        ],
        compiler_params=pltpu.CompilerParams(
            dimension_semantics=("arbitrary", "arbitrary"), vmem_limit_bytes=VMEM_LIMIT),
        name="attn",
    )(u3, u3, u3, c_t, sa_t, sb_t, qnw, knw, lam4, subln_w)


def _mixer_body(x_ref, og_ref, od_ref, ga_ref, gb_ref, bg_ref, wg_ref, wd_ref, wo_ref, o_ref):
    d = x_ref.shape[1]
    ya = jnp.dot(og_ref[...], wg_ref[...], preferred_element_type=F32)
    yb = jnp.dot(od_ref[...], wd_ref[...], preferred_element_type=F32)
    ga = _sigmoid(ga_ref[...].astype(F32) + bg_ref[:, :d])
    gb = _sigmoid(gb_ref[...].astype(F32) + bg_ref[:, d:])
    mix = (ga * ya + gb * yb).astype(BF16)
    o_ref[...] = x_ref[...] + jnp.dot(mix, wo_ref[...], preferred_element_type=F32)


def _mixer(x2d, og, od, u, b_gate, wg, wd, wo, gate_col, tm):
    m, d = x2d.shape
    row = lambda i: (i, 0)
    full = lambda i: (0, 0)
    return pl.pallas_call(
        _mixer_body,
        grid=(m // tm,),
        in_specs=[
            pl.BlockSpec((tm, d), row),
            pl.BlockSpec((tm, d), row),
            pl.BlockSpec((tm, d), row),
            pl.BlockSpec((tm, d), lambda i: (i, gate_col)),
            pl.BlockSpec((tm, d), lambda i: (i, gate_col + 1)),
            pl.BlockSpec((1, 2 * d), full),
            pl.BlockSpec((d, d), full),
            pl.BlockSpec((d, d), full),
            pl.BlockSpec((d, d), full),
        ],
        out_specs=pl.BlockSpec((tm, d), row),
        out_shape=jax.ShapeDtypeStruct((m, d), F32),
        compiler_params=pltpu.CompilerParams(
            dimension_semantics=("arbitrary",), vmem_limit_bytes=VMEM_LIMIT),
        name="mixer",
    )(x2d, og, od, u, u, b_gate, wg, wd, wo)


def _ffn_body(x_ref, n2_ref, wup_ref, cw_ref, cb_ref, wdn_ref, o_ref, carry_scr, *, tpb, tf):
    i = pl.program_id(0)
    dff = wdn_ref.shape[0]
    x = x_ref[...]
    ms = jnp.mean(x * x, axis=-1, keepdims=True)
    hb = (x * lax.rsqrt(ms + EPS) * n2_ref[...]).astype(BF16)
    tm = x.shape[0]
    first = (i % tpb) == 0

    def conv(u, cols):
        prev = jnp.where(first, 0.0, carry_scr[:, cols])
        w = cw_ref[:, cols]
        acc = u * w[FFN_CONV - 1:FFN_CONV] + cb_ref[:, cols]
        for s in range(1, FFN_CONV):
            acc = acc + _shift_rows(u, prev, s) * w[FFN_CONV - 1 - s:FFN_CONV - s]
        carry_scr[:, cols] = u[tm - 8:tm]
        return acc

    acc = x
    for f in range(dff // tf):
        gcols = slice(f * tf, (f + 1) * tf)
        vcols = slice(dff + f * tf, dff + (f + 1) * tf)
        ug = jnp.dot(hb, wup_ref[:, gcols], preferred_element_type=F32)
        uv = jnp.dot(hb, wup_ref[:, vcols], preferred_element_type=F32)
        act = (_silu(conv(ug, gcols)) * conv(uv, vcols)).astype(BF16)
        acc = acc + jnp.dot(act, wdn_ref[gcols, :], preferred_element_type=F32)
    o_ref[...] = acc


def _ffn(x2d, n2, wup, cw, cb, wdn, seq, tm, tf):
    m, d = x2d.shape
    dff = wdn.shape[0]
    full = lambda i: (0, 0)
    body = functools.partial(_ffn_body, tpb=seq // tm, tf=tf)
    return pl.pallas_call(
        body,
        grid=(m // tm,),
        in_specs=[
            pl.BlockSpec((tm, d), lambda i: (i, 0)),
            pl.BlockSpec((1, d), full),
            pl.BlockSpec((d, 2 * dff), full, pipeline_mode=pl.Buffered(1)),
            pl.BlockSpec((FFN_CONV, 2 * dff), full),
            pl.BlockSpec((1, 2 * dff), full),
            pl.BlockSpec((dff, d), full, pipeline_mode=pl.Buffered(1)),
        ],
        out_specs=pl.BlockSpec((tm, d), lambda i: (i, 0)),
        out_shape=jax.ShapeDtypeStruct((m, d), F32),
        scratch_shapes=[pltpu.VMEM((8, 2 * dff), F32)],
        compiler_params=pltpu.CompilerParams(
            dimension_semantics=("arbitrary",), vmem_limit_bytes=VMEM_LIMIT),
        name="ffn",
    )(x2d, n2, wup, cw, cb, wdn)


def _rope_tables(seq):
    half = ROPE_DIM // 2
    pos = jnp.arange(seq, dtype=F32)
    inv_freq = ROPE_THETA ** (-jnp.arange(0, ROPE_DIM, 2, dtype=F32) / ROPE_DIM)
    ang = pos[:, None] * inv_freq[None, :]
    cos, sin = jnp.cos(ang), jnp.sin(ang)
    ones = jnp.ones((seq, DIFF_D - ROPE_DIM), F32)
    zeros = jnp.zeros((seq, DIFF_D - ROPE_DIM), F32)
    zh = jnp.zeros((seq, half), F32)
    c = jnp.concatenate([cos, cos, ones], axis=-1)
    sa = jnp.concatenate([-sin, zh, zeros], axis=-1)
    sb = jnp.concatenate([zh, sin, zeros], axis=-1)
    return tuple(jnp.concatenate([t, t], axis=-1) for t in (c, sa, sb))


def _pad_lanes(v, offset):
    return jnp.zeros((LANES,), F32).at[offset:offset + v.shape[0]].set(v.astype(F32))


def kernel(x, norm1_w, w_in, b_gate, gdn_conv_w, gdn_A_log, gdn_dt_bias, gdn_norm_w,
           diff_q_norm_w, diff_k_norm_w, lambda_q1, lambda_k1, lambda_q2, lambda_k2,
           diff_subln_w, w_gdn_out, w_diff_out, w_o, norm2_w, w_up, ffn_conv_w,
           ffn_conv_b, w_down):
    b, seq, d = x.shape
    depth = norm1_w.shape[0]
    gqk = GDN_HEADS * GDN_D
    dqk = DIFF_HEADS * 2 * DIFF_D
    o_b = 4 * gqk
    o_dq = o_b + 2 * GDN_HEADS
    o_gate = o_dq + 3 * dqk
    dff = w_down.shape[1]
    rope_tabs = _rope_tables(seq)
    x2d = x.reshape(b * seq, d)
    tm_in = min(2048, seq)
    tq = min(256, seq)
    tm_mix = min(512, seq)
    tm_ffn = min(512, seq)
    tf = dff

    for layer in range(depth):
        lambda_init = 0.8 - 0.6 * math.exp(-0.3 * layer)
        wl = w_in[layer]
        wl = wl.astype(BF16)
        w_conv = wl[:, :3 * gqk]
        w_plain = jnp.concatenate([wl[:, 3 * gqk:o_b], wl[:, o_dq:]], axis=1)
        w_ba = jnp.pad(wl[:, o_b:o_dq], ((0, 0), (0, LANES - 2 * GDN_HEADS)))
        gp = jnp.stack([_pad_lanes(gdn_A_log[layer], GDN_HEADS),
                        _pad_lanes(gdn_dt_bias[layer], GDN_HEADS)])
        uc, up, bgc, gt = _inproj(x2d, norm1_w[layer][None, :], w_conv, w_plain, w_ba, gp,
                                  gdn_conv_w[layer], seq, tm_in, INPROJ_STEPS)
        up3 = up.reshape(b, seq, up.shape[1])
        og = _gdn(uc.reshape(b, seq, uc.shape[1]), up3, bgc.reshape(b, seq, LANES), gt,
                  gdn_norm_w[layer][None, :], hb=4)
        qnw = jnp.tile(diff_q_norm_w[layer], 2)[None, :]
        knw = jnp.tile(diff_k_norm_w[layer], 2)[None, :]
        lam4 = jnp.stack([jnp.pad(v[layer], (0, LANES - DIFF_D))
                          for v in (lambda_q1, lambda_k1, lambda_q2, lambda_k2)])
        od = _attn(up3, rope_tabs, qnw, knw, lam4, diff_subln_w[layer][None, :], lambda_init, tq)
        x2d = _mixer(x2d, og.reshape(b * seq, gqk), od.reshape(b * seq, dqk), up,
                     b_gate[layer][None, :], w_gdn_out[layer].astype(BF16),
                     w_diff_out[layer].astype(BF16), w_o[layer].astype(BF16),
                     (gqk + 3 * dqk) // d, tm_mix)
        x2d = _ffn(x2d, norm2_w[layer][None, :], w_up[layer].astype(BF16), ffn_conv_w[layer],
                   ffn_conv_b[layer][None, :], w_down[layer].astype(BF16), seq, tm_ffn, tf)
    return x2d.reshape(b, seq, d)
```
